```python
import math
import jax, jax.numpy as jnp
from jax import lax
import numpy as np

D_MODEL = 2048
BATCH = 8
SEQ = 2048
DEPTH = 2

MEM_LEN = 256
CHUNK = 128
Q_BLOCK = 128
D_A = D_MODEL // 2
A_GROUPS = 8
A_GROUP_DIM = D_A // A_GROUPS
D_B = D_MODEL // 4
B_HEADS = 4
B_HEAD_DIM = D_B // B_HEADS
D_C = D_MODEL // 4
C_HEADS = 4
C_HEAD_DIM = D_C // C_HEADS
SPLIT_SIZES = (D_A, D_A, D_A, D_B, D_B, D_B, D_B, D_C, D_C)
IN_WIDTH = sum(SPLIT_SIZES)
EPS = 1e-6

kernel_name = "hybrid_sgu_stickbreak_memxattn"


def rms_norm(x, g):
    xf = x.astype(jnp.float32)
    y = xf * lax.rsqrt(jnp.mean(xf * xf, axis=-1, keepdims=True) + EPS)
    return (y * g.astype(jnp.float32)).astype(x.dtype)


def layer_norm(x, g, b):
    xf = x.astype(jnp.float32)
    mu = jnp.mean(xf, axis=-1, keepdims=True)
    xc = xf - mu
    y = xc * lax.rsqrt(jnp.mean(xc * xc, axis=-1, keepdims=True) + EPS)
    return (y * g.astype(jnp.float32) + b.astype(jnp.float32)).astype(x.dtype)


def sgu_mixer(u, v, ln_g, ln_b, w_s, b_s):
    bsz, s_len, _ = v.shape
    n_chunks = s_len // CHUNK
    v = layer_norm(v, ln_g, ln_b)
    vc = v.reshape(bsz, n_chunks, CHUNK, A_GROUPS, A_GROUP_DIM)
    mask = jnp.tril(jnp.ones((CHUNK, CHUNK), dtype=bool))
    w = jnp.where(mask[None], w_s, jnp.zeros_like(w_s))
    mixed = jnp.einsum('gts,bcsgd->bctgd', w, vc) + b_s.T[None, None, :, :, None]
    return u * mixed.reshape(bsz, s_len, D_A)


def stick_breaking_attention(q, k, v):
    s_len = q.shape[1]
    scale = 1.0 / math.sqrt(q.shape[-1])
    outs = []
    for i in range(s_len // Q_BLOCK):
        start = i * Q_BLOCK
        kv_len = start + Q_BLOCK
        q_blk = q[:, start:kv_len]
        k_blk = k[:, :kv_len]
        v_blk = v[:, :kv_len]
        z = jnp.einsum('bthd,bshd->bhts', q_blk, k_blk).astype(jnp.float32) * scale
        t_idx = start + jnp.arange(Q_BLOCK)[:, None]
        s_idx = jnp.arange(kv_len)[None, :]
        causal = s_idx < t_idx
        log_beta = jax.nn.log_sigmoid(z)
        log_1mb = jnp.where(causal, jax.nn.log_sigmoid(-z), 0.0)
        rc = lax.cumsum(log_1mb, axis=3, reverse=True)
        after = jnp.pad(rc[..., 1:], ((0, 0), (0, 0), (0, 0), (0, 1)))
        a = jnp.where(causal, jnp.exp(log_beta + after), 0.0)
        outs.append(jnp.einsum('bhts,bshd->bthd', a.astype(v.dtype), v_blk))
    return jnp.concatenate(outs, axis=1)


def memory_attention(q, mem_k, mem_v, q_g, k_g):
    scale = 1.0 / math.sqrt(q.shape[-1])
    qn = rms_norm(q, q_g)
    kn = rms_norm(mem_k, k_g)
    s = jnp.einsum('bthd,bmhd->bhtm', qn, kn).astype(jnp.float32) * scale
    p = jax.nn.softmax(s, axis=-1)
    return jnp.einsum('bhtm,bmhd->bthd', p.astype(mem_v.dtype), mem_v)


def setup_inputs(seed: int = 0) -> dict:
    key = jax.random.key(seed)
    ks = jax.random.split(key, 16)
    f32 = jnp.float32
    x = jax.random.normal(ks[0], (BATCH, SEQ, D_MODEL), f32)
    mem = jax.random.normal(ks[1], (BATCH, MEM_LEN, D_MODEL), f32)
    norm_g = 1.0 + 0.01 * jax.random.normal(ks[2], (DEPTH, D_MODEL), f32)
    w_in = jax.random.normal(ks[3], (DEPTH, D_MODEL, IN_WIDTH), f32) * D_MODEL ** -0.5
    sgu_ln_g = 1.0 + 0.01 * jax.random.normal(ks[4], (DEPTH, D_A), f32)
    sgu_ln_b = 0.01 * jax.random.normal(ks[5], (DEPTH, D_A), f32)
    sgu_w = jax.random.normal(ks[6], (DEPTH, A_GROUPS, CHUNK, CHUNK), f32) * CHUNK ** -0.5
    sgu_b = 1.0 + 0.01 * jax.random.normal(ks[7], (DEPTH, A_GROUPS, CHUNK), f32)
    mem_norm_g = 1.0 + 0.01 * jax.random.normal(ks[8], (DEPTH, D_MODEL), f32)
    w_mem_kv = jax.random.normal(ks[9], (DEPTH, D_MODEL, 2 * D_C), f32) * D_MODEL ** -0.5
    q_norm_g = 1.0 + 0.01 * jax.random.normal(ks[10], (DEPTH, C_HEAD_DIM), f32)
    k_norm_g = 1.0 + 0.01 * jax.random.normal(ks[11], (DEPTH, C_HEAD_DIM), f32)
    w_out = jax.random.normal(ks[12], (DEPTH, D_MODEL, D_MODEL), f32) * D_MODEL ** -0.5
    return {"x": x, "mem": mem, "norm_g": norm_g, "w_in": w_in,
            "sgu_ln_g": sgu_ln_g, "sgu_ln_b": sgu_ln_b, "sgu_w": sgu_w, "sgu_b": sgu_b,
            "mem_norm_g": mem_norm_g, "w_mem_kv": w_mem_kv,
            "q_norm_g": q_norm_g, "k_norm_g": k_norm_g, "w_out": w_out}


def reference(x, mem, norm_g, w_in, sgu_ln_g, sgu_ln_b, sgu_w, sgu_b,
              mem_norm_g, w_mem_kv, q_norm_g, k_norm_g, w_out):
    bsz, s_len, _ = x.shape
    split_idx = list(np.cumsum(SPLIT_SIZES)[:-1])
    for l in range(DEPTH):
        h = rms_norm(x, norm_g[l])
        proj = jnp.matmul(h, w_in[l])
        u_a, v_a, z_a, q_b, k_b, v_b, z_b, q_c, z_c = jnp.split(proj, split_idx, axis=-1)

        u_a = jax.nn.gelu(u_a, approximate=False)
        v_a = jax.nn.gelu(v_a, approximate=False)
        y_a = sgu_mixer(u_a, v_a, sgu_ln_g[l], sgu_ln_b[l], sgu_w[l], sgu_b[l]) * jax.nn.silu(z_a)

        hb = (bsz, s_len, B_HEADS, B_HEAD_DIM)
        y_b = stick_breaking_attention(q_b.reshape(hb), k_b.reshape(hb), v_b.reshape(hb))
        y_b = y_b.reshape(bsz, s_len, D_B) * jax.nn.silu(z_b)

        mem_h = rms_norm(mem, mem_norm_g[l])
        mem_kv = jnp.matmul(mem_h, w_mem_kv[l])
        mem_k, mem_v = jnp.split(mem_kv, 2, axis=-1)
        hm = (bsz, mem.shape[1], C_HEADS, C_HEAD_DIM)
        y_c = memory_attention(q_c.reshape(bsz, s_len, C_HEADS, C_HEAD_DIM),
                               mem_k.reshape(hm), mem_v.reshape(hm), q_norm_g[l], k_norm_g[l])
        y_c = y_c.reshape(bsz, s_len, D_C) * jax.nn.silu(z_c)

        y = jnp.concatenate([y_a, y_b, y_c], axis=-1)
        x = x + jnp.matmul(y, w_out[l])
    return x
```

```python
import functools
import math

import jax
import jax.numpy as jnp
from jax import lax
from jax.experimental import pallas as pl
from jax.experimental.pallas import tpu as pltpu

D_MODEL = 2048
MEM_LEN = 256
CHUNK = 128
D_A = D_MODEL // 2
A_GROUPS = 8
A_GROUP_DIM = D_A // A_GROUPS
D_B = D_MODEL // 4
B_HEADS = 4
B_HEAD_DIM = D_B // B_HEADS
D_C = D_MODEL // 4
C_HEADS = 4
C_HEAD_DIM = D_C // C_HEADS
IN_WIDTH = 3 * D_A + 4 * D_B + 2 * D_C
EPS = 1e-6

F32 = jnp.float32
BF16 = jnp.bfloat16

VMEM_LIMIT_BYTES = 56 * 1024 * 1024

_NT = (((1,), (1,)), ((), ()))


def _gelu(x):
    return 0.5 * x * (1.0 + lax.erf(x * math.sqrt(0.5)))


def _silu(x):
    return x / (1.0 + jnp.exp(-x))


def _inproj_kernel(x_ref, g_ref, w_ref, o_ref, h_ref):
    @pl.when(pl.program_id(1) == 0)
    def _():
        x = x_ref[...]
        ms = jnp.mean(x * x, axis=-1, keepdims=True)
        h_ref[...] = (x * lax.rsqrt(ms + EPS) * g_ref[...]).astype(BF16)

    o_ref[...] = jnp.dot(h_ref[...], w_ref[...],
                         preferred_element_type=F32).astype(o_ref.dtype)


def _inproj(x2d, g, w_bf16, *, tm=1024, tn=1024):
    m, k = x2d.shape
    n = w_bf16.shape[1]
    return pl.pallas_call(
        _inproj_kernel,
        grid=(m // tm, n // tn),
        in_specs=[
            pl.BlockSpec((tm, k), lambda i, j: (i, 0)),
            pl.BlockSpec((1, k), lambda i, j: (0, 0)),
            pl.BlockSpec((k, tn), lambda i, j: (0, j)),
        ],
        out_specs=pl.BlockSpec((tm, tn), lambda i, j: (i, j)),
        out_shape=jax.ShapeDtypeStruct((m, n), BF16),
        scratch_shapes=[pltpu.VMEM((tm, k), BF16)],
        compiler_params=pltpu.CompilerParams(
            dimension_semantics=("arbitrary", "arbitrary"),
            vmem_limit_bytes=VMEM_LIMIT_BYTES),
        name="inproj",
    )(x2d, g, w_bf16)


def _memkv_kernel(mem_ref, g_ref, w_ref, kg_ref, k_ref, v_ref):
    x = mem_ref[0]
    ms = jnp.mean(x * x, axis=-1, keepdims=True)
    h = (x * lax.rsqrt(ms + EPS) * g_ref[0]).astype(BF16)
    kv = jnp.dot(h, w_ref[0], preferred_element_type=F32)
    for hd in range(C_HEADS):
        sl = slice(hd * C_HEAD_DIM, (hd + 1) * C_HEAD_DIM)
        kh = kv[:, sl]
        kms = jnp.mean(kh * kh, axis=-1, keepdims=True)
        k_ref[0, 0, :, sl] = (kh * lax.rsqrt(kms + EPS) * kg_ref[0]).astype(BF16)
    v_ref[0, 0] = kv[:, D_C:].astype(BF16)


def _memkv(mem, mem_norm_g, w_mem_kv_bf16, k_norm_g):
    bsz = mem.shape[0]
    depth = w_mem_kv_bf16.shape[0]
    out = jax.ShapeDtypeStruct((depth, bsz, MEM_LEN, D_C), BF16)
    return pl.pallas_call(
        _memkv_kernel,
        grid=(depth, bsz),
        in_specs=[
            pl.BlockSpec((1, MEM_LEN, D_MODEL), lambda l, b: (b, 0, 0)),
            pl.BlockSpec((1, 1, D_MODEL), lambda l, b: (l, 0, 0)),
            pl.BlockSpec((1, D_MODEL, 2 * D_C), lambda l, b: (l, 0, 0)),
            pl.BlockSpec((1, 1, C_HEAD_DIM), lambda l, b: (l, 0, 0)),
        ],
        out_specs=[
            pl.BlockSpec((1, 1, MEM_LEN, D_C), lambda l, b: (l, b, 0, 0)),
            pl.BlockSpec((1, 1, MEM_LEN, D_C), lambda l, b: (l, b, 0, 0)),
        ],
        out_shape=[out, out],
        compiler_params=pltpu.CompilerParams(
            dimension_semantics=("arbitrary", "arbitrary"),
            vmem_limit_bytes=VMEM_LIMIT_BYTES),
        name="memkv",
    )(mem, mem_norm_g.reshape(depth, 1, D_MODEL), w_mem_kv_bf16,
      k_norm_g.reshape(depth, 1, C_HEAD_DIM))


def _sb_tile(q, k_t, v_t, carry, acc, tri, causal):
    z = lax.dot_general(q, k_t, _NT, preferred_element_type=F32) * (1.0 / math.sqrt(B_HEAD_DIM))
    e = jnp.log1p(jnp.exp(-jnp.abs(z)))
    log_beta = jnp.minimum(z, 0.0) - e
    log_1mb = log_beta - z
    if causal is not None:
        log_1mb = jnp.where(causal, log_1mb, 0.0)
    hi = log_1mb.astype(BF16)
    lo = (log_1mb - hi.astype(F32)).astype(BF16)
    r = jnp.dot(jnp.concatenate([hi, lo], axis=1), tri, preferred_element_type=F32)
    after = r[:, :CHUNK] + carry
    a = jnp.exp(log_beta + after)
    if causal is not None:
        a = jnp.where(causal, a, 0.0)
    acc = acc + jnp.dot(a.astype(BF16), v_t, preferred_element_type=F32)
    carry = carry + r[:, CHUNK:]
    return carry, acc


def _mixer_kernel(u_ref, v_ref, za_ref, qb_ref, kb_ref, vb_ref, zb_ref, qc_ref, zc_ref,
                  mk_ref, mv_ref, lng_ref, lnb_ref, w_ref, bias_ref, qg_ref, tri_ref,
                  y_ref):
    i = pl.program_id(1)

    v = _gelu(v_ref[0].astype(F32))
    mu = jnp.mean(v, axis=-1, keepdims=True)
    vc = v - mu
    var = jnp.mean(vc * vc, axis=-1, keepdims=True)
    vn = (vc * lax.rsqrt(var + EPS) * lng_ref[...] + lnb_ref[...]).astype(BF16)
    for g in range(A_GROUPS):
        sl = slice(g * A_GROUP_DIM, (g + 1) * A_GROUP_DIM)
        mixed = jnp.dot(w_ref[g], vn[:, sl], preferred_element_type=F32) + bias_ref[:, sl]
        u = _gelu(u_ref[0, :, sl].astype(F32))
        y_ref[0, :, sl] = (u * mixed * _silu(za_ref[0, :, sl].astype(F32))).astype(y_ref.dtype)

    row = lax.broadcasted_iota(jnp.int32, (CHUNK, CHUNK), 0)
    col = lax.broadcasted_iota(jnp.int32, (CHUNK, CHUNK), 1)
    causal = col < row
    tri = tri_ref[...]
    qs = [qb_ref[0, :, h * B_HEAD_DIM:(h + 1) * B_HEAD_DIM] for h in range(B_HEADS)]

    def kv_tile(j, h):
        start = pl.multiple_of(j * CHUNK, CHUNK)
        sl = slice(h * B_HEAD_DIM, (h + 1) * B_HEAD_DIM)
        return kb_ref[0, pl.ds(start, CHUNK), sl], vb_ref[0, pl.ds(start, CHUNK), sl]

    zero = jnp.zeros((CHUNK, B_HEAD_DIM), F32)
    state = []
    for h in range(B_HEADS):
        k_t, v_t = kv_tile(i, h)
        state.extend(_sb_tile(qs[h], k_t, v_t, zero, zero, tri, causal))

    def body(jj, st):
        j = i - 1 - jj
        out = []
        for h in range(B_HEADS):
            k_t, v_t = kv_tile(j, h)
            out.extend(_sb_tile(qs[h], k_t, v_t, st[2 * h], st[2 * h + 1], tri, None))
        return tuple(out)

    state = lax.fori_loop(0, i, body, tuple(state))
    for h in range(B_HEADS):
        sl = slice(h * B_HEAD_DIM, (h + 1) * B_HEAD_DIM)
        yb = state[2 * h + 1] * _silu(zb_ref[0, :, sl].astype(F32))
        y_ref[0, :, D_A + h * B_HEAD_DIM:D_A + (h + 1) * B_HEAD_DIM] = yb.astype(y_ref.dtype)

    for h in range(C_HEADS):
        sl = slice(h * C_HEAD_DIM, (h + 1) * C_HEAD_DIM)
        q = qc_ref[0, :, sl].astype(F32)
        qms = jnp.mean(q * q, axis=-1, keepdims=True)
        qn = (q * lax.rsqrt(qms + EPS) * qg_ref[...]).astype(BF16)
        s = lax.dot_general(qn, mk_ref[0, 0, :, sl], _NT,
                            preferred_element_type=F32) * (1.0 / math.sqrt(C_HEAD_DIM))
        p = jnp.exp(s - jnp.max(s, axis=-1, keepdims=True))
        denom = jnp.sum(p, axis=-1, keepdims=True)
        o = jnp.dot(p.astype(BF16), mv_ref[0, 0, :, sl], preferred_element_type=F32) / denom
        yc = o * _silu(zc_ref[0, :, sl].astype(F32))
        y_ref[0, :, D_A + D_B + h * C_HEAD_DIM:D_A + D_B + (h + 1) * C_HEAD_DIM] = yc.astype(y_ref.dtype)


def _mixer(proj, memk, memv, layer, lng, lnb, w_tril, bias_full, qg, tri):
    bsz, s_len, _ = proj.shape
    nq = s_len // CHUNK
    a_blk, b_blk = D_A, D_B
    off_b = 3 * D_A // b_blk

    def row_spec(width, col):
        return pl.BlockSpec((1, CHUNK, width), lambda b, i, c=col: (b, i, c))

    def seq_spec(width, col):
        return pl.BlockSpec((1, s_len, width), lambda b, i, c=col: (b, 0, c))

    def const_spec(shape):
        nd = len(shape)
        return pl.BlockSpec(shape, lambda b, i, nd=nd: (0,) * nd)

    mem_spec = pl.BlockSpec((1, 1, MEM_LEN, D_C), lambda b, i, l=layer: (l, b, 0, 0))
    return pl.pallas_call(
        _mixer_kernel,
        grid=(bsz, nq),
        in_specs=[
            row_spec(a_blk, 0), row_spec(a_blk, 1), row_spec(a_blk, 2),
            row_spec(b_blk, off_b), seq_spec(b_blk, off_b + 1),
            seq_spec(b_blk, off_b + 2), row_spec(b_blk, off_b + 3),
            row_spec(b_blk, off_b + 4), row_spec(b_blk, off_b + 5),
            mem_spec, mem_spec,
            const_spec((1, D_A)), const_spec((1, D_A)),
            const_spec((A_GROUPS, CHUNK, CHUNK)), const_spec((CHUNK, D_A)),
            const_spec((1, C_HEAD_DIM)), const_spec((2 * CHUNK, 2 * CHUNK)),
        ],
        out_specs=pl.BlockSpec((1, CHUNK, D_MODEL), lambda b, i: (b, i, 0)),
        out_shape=jax.ShapeDtypeStruct((bsz, s_len, D_MODEL), BF16),
        compiler_params=pltpu.CompilerParams(
            dimension_semantics=("arbitrary", "arbitrary"),
            vmem_limit_bytes=VMEM_LIMIT_BYTES),
        name="mixer",
    )(proj, proj, proj, proj, proj, proj, proj, proj, proj, memk, memv,
      lng, lnb, w_tril, bias_full, qg, tri)


def _outproj_kernel(y_ref, w_ref, x_ref, o_ref):
    o_ref[...] = x_ref[...] + jnp.dot(y_ref[...], w_ref[...], preferred_element_type=F32)


def _outproj(y2d, w_bf16, x2d, *, tm=1024, tn=1024):
    m, k = y2d.shape
    n = w_bf16.shape[1]
    return pl.pallas_call(
        _outproj_kernel,
        grid=(m // tm, n // tn),
        in_specs=[
            pl.BlockSpec((tm, k), lambda i, j: (i, 0)),
            pl.BlockSpec((k, tn), lambda i, j: (0, j)),
            pl.BlockSpec((tm, tn), lambda i, j: (i, j)),
        ],
        out_specs=pl.BlockSpec((tm, tn), lambda i, j: (i, j)),
        out_shape=jax.ShapeDtypeStruct((m, n), F32),
        compiler_params=pltpu.CompilerParams(
            dimension_semantics=("arbitrary", "arbitrary"),
            vmem_limit_bytes=VMEM_LIMIT_BYTES),
        name="outproj",
    )(y2d, w_bf16, x2d)


def _cumsum_matrix():
    j = lax.broadcasted_iota(jnp.int32, (CHUNK, CHUNK), 0)
    s = lax.broadcasted_iota(jnp.int32, (CHUNK, CHUNK), 1)
    half = jnp.concatenate([(j > s).astype(BF16), jnp.ones((CHUNK, CHUNK), BF16)], axis=1)
    return jnp.concatenate([half, half], axis=0)


def kernel(x, mem, norm_g, w_in, sgu_ln_g, sgu_ln_b, sgu_w, sgu_b, mem_norm_g, w_mem_kv,
           q_norm_g, k_norm_g, w_out):
    bsz, s_len, _ = x.shape
    depth = w_in.shape[0]
    assert s_len % CHUNK == 0 and x.shape[2] == D_MODEL and mem.shape[1] == MEM_LEN

    tri = _cumsum_matrix()
    tril = jnp.tril(jnp.ones((CHUNK, CHUNK), dtype=bool))
    memk, memv = _memkv(mem, mem_norm_g, w_mem_kv.astype(BF16), k_norm_g)

    x2d = x.reshape(bsz * s_len, D_MODEL)
    for l in range(depth):
        proj = _inproj(x2d, norm_g[l].reshape(1, D_MODEL), w_in[l].astype(BF16))
        w_tril = jnp.where(tril[None], sgu_w[l], 0.0).astype(BF16)
        bias_full = jnp.repeat(sgu_b[l].T, A_GROUP_DIM, axis=1)
        y = _mixer(proj.reshape(bsz, s_len, IN_WIDTH), memk, memv, l,
                   sgu_ln_g[l].reshape(1, D_A), sgu_ln_b[l].reshape(1, D_A),
                   w_tril, bias_full, q_norm_g[l].reshape(1, C_HEAD_DIM), tri)
        x2d = _outproj(y.reshape(bsz * s_len, D_MODEL), w_out[l].astype(BF16), x2d)
    return x2d.reshape(bsz, s_len, D_MODEL)
```

```python
import math

import jax
import jax.numpy as jnp
from jax import lax
from jax.experimental import pallas as pl
from jax.experimental.pallas import tpu as pltpu

D_MODEL = 2048
MEM_LEN = 256
CHUNK = 128
D_A = D_MODEL // 2
A_GROUPS = 8
A_GROUP_DIM = D_A // A_GROUPS
D_B = D_MODEL // 4
B_HEADS = 4
B_HEAD_DIM = D_B // B_HEADS
D_C = D_MODEL // 4
C_HEADS = 4
C_HEAD_DIM = D_C // C_HEADS
IN_WIDTH = 3 * D_A + 4 * D_B + 2 * D_C
EPS = 1e-6
LOG2E = math.log2(math.e)

SB_Q_TILES = 4
SB_Q_BLOCK = SB_Q_TILES * CHUNK

F32 = jnp.float32
BF16 = jnp.bfloat16

VMEM_LIMIT_BYTES = 56 * 1024 * 1024

_NT = (((1,), (1,)), ((), ()))


def _gelu(x):
    return 0.5 * x * (1.0 + lax.erf(x * math.sqrt(0.5)))


def _silu(x):
    return x / (1.0 + jnp.exp(-x))


def _params(n_grid_dims):
    return pltpu.CompilerParams(
        dimension_semantics=("arbitrary",) * n_grid_dims,
        vmem_limit_bytes=VMEM_LIMIT_BYTES)


def _inproj_kernel(x_ref, g_ref, w_ref, cs_ref, o_ref, h_ref):
    @pl.when(pl.program_id(1) == 0)
    def _():
        x = x_ref[...]
        ms = jnp.mean(x * x, axis=-1, keepdims=True)
        h_ref[...] = (x * lax.rsqrt(ms + EPS) * g_ref[...]).astype(BF16)

    acc = jnp.dot(h_ref[...], w_ref[...], preferred_element_type=F32)
    o_ref[...] = (acc * cs_ref[...]).astype(o_ref.dtype)


def _inproj(x2d, g, w_bf16, col_scale, *, tm=1024, tn=1024):
    m, k = x2d.shape
    n = w_bf16.shape[1]
    return pl.pallas_call(
        _inproj_kernel,
        grid=(m // tm, n // tn),
        in_specs=[
            pl.BlockSpec((tm, k), lambda i, j: (i, 0)),
            pl.BlockSpec((1, k), lambda i, j: (0, 0)),
            pl.BlockSpec((k, tn), lambda i, j: (0, j)),
            pl.BlockSpec((1, tn), lambda i, j: (0, j)),
        ],
        out_specs=pl.BlockSpec((tm, tn), lambda i, j: (i, j)),
        out_shape=jax.ShapeDtypeStruct((m, n), BF16),
        scratch_shapes=[pltpu.VMEM((tm, k), BF16)],
        compiler_params=_params(2),
        name="inproj",
    )(x2d, g, w_bf16, col_scale)


def _memkv_kernel(mem_ref, g_ref, w_ref, kg_ref, k_ref, v_ref):
    x = mem_ref[0]
    ms = jnp.mean(x * x, axis=-1, keepdims=True)
    h = (x * lax.rsqrt(ms + EPS) * g_ref[0]).astype(BF16)
    kv = jnp.dot(h, w_ref[0], preferred_element_type=F32)
    for hd in range(C_HEADS):
        sl = slice(hd * C_HEAD_DIM, (hd + 1) * C_HEAD_DIM)
        kh = kv[:, sl]
        kms = jnp.mean(kh * kh, axis=-1, keepdims=True)
        k_ref[0, 0, :, sl] = (kh * lax.rsqrt(kms + EPS) * kg_ref[0]).astype(BF16)
    v_ref[0, 0] = kv[:, D_C:].astype(BF16)


def _memkv(mem, mem_norm_g, w_mem_kv_bf16, k_norm_g):
    bsz = mem.shape[0]
    depth = w_mem_kv_bf16.shape[0]
    out = jax.ShapeDtypeStruct((depth, bsz, MEM_LEN, D_C), BF16)
    return pl.pallas_call(
        _memkv_kernel,
        grid=(depth, bsz),
        in_specs=[
            pl.BlockSpec((1, MEM_LEN, D_MODEL), lambda l, b: (b, 0, 0)),
            pl.BlockSpec((1, 1, D_MODEL), lambda l, b: (l, 0, 0)),
            pl.BlockSpec((1, D_MODEL, 2 * D_C), lambda l, b: (l, 0, 0)),
            pl.BlockSpec((1, 1, C_HEAD_DIM), lambda l, b: (l, 0, 0)),
        ],
        out_specs=[
            pl.BlockSpec((1, 1, MEM_LEN, D_C), lambda l, b: (l, b, 0, 0)),
            pl.BlockSpec((1, 1, MEM_LEN, D_C), lambda l, b: (l, b, 0, 0)),
        ],
        out_shape=[out, out],
        compiler_params=_params(2),
        name="memkv",
    )(mem, mem_norm_g.reshape(depth, 1, D_MODEL), w_mem_kv_bf16,
      k_norm_g.reshape(depth, 1, C_HEAD_DIM))


def _mixer_kernel(u_ref, v_ref, za_ref, qc_ref, zc_ref, mk_ref, mv_ref,
                  lng_ref, lnb_ref, w_ref, bias_ref, qg_ref, ya_ref, yc_ref):
    v = _gelu(v_ref[0].astype(F32))
    mu = jnp.mean(v, axis=-1, keepdims=True)
    vc = v - mu
    var = jnp.mean(vc * vc, axis=-1, keepdims=True)
    vn = (vc * lax.rsqrt(var + EPS) * lng_ref[...] + lnb_ref[...]).astype(BF16)
    for g in range(A_GROUPS):
        sl = slice(g * A_GROUP_DIM, (g + 1) * A_GROUP_DIM)
        mixed = jnp.dot(w_ref[g], vn[:, sl], preferred_element_type=F32) + bias_ref[:, sl]
        u = _gelu(u_ref[0, :, sl].astype(F32))
        ya_ref[0, :, sl] = (u * mixed * _silu(za_ref[0, :, sl].astype(F32))).astype(ya_ref.dtype)

    for h in range(C_HEADS):
        sl = slice(h * C_HEAD_DIM, (h + 1) * C_HEAD_DIM)
        q = qc_ref[0, :, sl].astype(F32)
        qms = jnp.mean(q * q, axis=-1, keepdims=True)
        qn = (q * lax.rsqrt(qms + EPS) * qg_ref[...]).astype(BF16)
        s = lax.dot_general(qn, mk_ref[0, 0, :, sl], _NT,
                            preferred_element_type=F32) * (1.0 / math.sqrt(C_HEAD_DIM))
        p = jnp.exp(s - jnp.max(s, axis=-1, keepdims=True))
        denom = jnp.sum(p, axis=-1, keepdims=True)
        o = jnp.dot(p.astype(BF16), mv_ref[0, 0, :, sl], preferred_element_type=F32) / denom
        yc_ref[0, :, sl] = (o * _silu(zc_ref[0, :, sl].astype(F32))).astype(yc_ref.dtype)


def _mixer(proj, memk, memv, layer, lng, lnb, w_tril, bias_full, qg):
    bsz, s_len, _ = proj.shape
    off_b = 3 * D_A // D_B

    def row_spec(width, col):
        return pl.BlockSpec((1, CHUNK, width), lambda b, i, c=col: (b, i, c))

    def const_spec(shape):
        nd = len(shape)
        return pl.BlockSpec(shape, lambda b, i, nd=nd: (0,) * nd)

    mem_spec = pl.BlockSpec((1, 1, MEM_LEN, D_C), lambda b, i, l=layer: (l, b, 0, 0))
    return pl.pallas_call(
        _mixer_kernel,
        grid=(bsz, s_len // CHUNK),
        in_specs=[
            row_spec(D_A, 0), row_spec(D_A, 1), row_spec(D_A, 2),
            row_spec(D_C, off_b + 4), row_spec(D_C, off_b + 5),
            mem_spec, mem_spec,
            const_spec((1, D_A)), const_spec((1, D_A)),
            const_spec((A_GROUPS, CHUNK, CHUNK)), const_spec((CHUNK, D_A)),
            const_spec((1, C_HEAD_DIM)),
        ],
        out_specs=[pl.BlockSpec((1, CHUNK, D_A), lambda b, i: (b, i, 0)),
                   pl.BlockSpec((1, CHUNK, D_C), lambda b, i: (b, i, 0))],
        out_shape=[jax.ShapeDtypeStruct((bsz, s_len, D_A), BF16),
                   jax.ShapeDtypeStruct((bsz, s_len, D_C), BF16)],
        compiler_params=_params(2),
        name="mixer",
    )(proj, proj, proj, proj, proj, memk, memv, lng, lnb, w_tril, bias_full, qg)


def _sb_kernel(q_ref, k_ref, v_ref, z_ref, tri_ref, y_ref, carry_ref, acc_ref, lb_ref, r_ref):
    blk = pl.program_id(1)
    carry_ref[...] = jnp.zeros_like(carry_ref)
    acc_ref[...] = jnp.zeros_like(acc_ref)
    row = lax.broadcasted_iota(jnp.int32, (CHUNK, CHUNK), 0)
    col = lax.broadcasted_iota(jnp.int32, (CHUNK, CHUNK), 1)
    causal = col < row
    heads = range(B_HEADS)
    sls = [slice(h * B_HEAD_DIM, (h + 1) * B_HEAD_DIM) for h in heads]

    def mask_top(x, r0, diag):
        if not diag:
            return x
        top = jnp.where(causal, x[:CHUNK], 0.0)
        return top if r0 + CHUNK == SB_Q_BLOCK else jnp.concatenate([top, x[CHUNK:]], axis=0)

    def front(j, r0, diag):
        start = pl.multiple_of(j * CHUNK, CHUNK)
        rows = slice(r0, SB_Q_BLOCK)
        zs = [lax.dot_general(q_ref[0, rows, sls[h]], k_ref[0, pl.ds(start, CHUNK), sls[h]], _NT,
                              preferred_element_type=F32) for h in heads]
        log_betas, hls = [], []
        for h in heads:
            z = zs[h]
            e = jnp.log(1.0 + jnp.exp2(-jnp.abs(z))) * LOG2E
            log_beta = jnp.minimum(z, 0.0) - e
            log_1mb = mask_top(log_beta - z, r0, diag)
            hi = log_1mb.astype(BF16)
            lo = (log_1mb - hi.astype(F32)).astype(BF16)
            log_betas.append(log_beta)
            hls.append(jnp.concatenate([hi, lo], axis=1))
        rs = [jnp.dot(hls[h], tri_ref[...], preferred_element_type=F32) for h in heads]
        return log_betas, rs

    def back(j, r0, diag, log_betas, rs):
        start = pl.multiple_of(j * CHUNK, CHUNK)
        rows = slice(r0, SB_Q_BLOCK)
        for h in heads:
            a = mask_top(jnp.exp2(log_betas[h] + rs[h][:, :CHUNK] + carry_ref[h, rows, :]), r0, diag)
            acc_ref[h, rows, :] += jnp.dot(a.astype(BF16), v_ref[0, pl.ds(start, CHUNK), sls[h]],
                                           preferred_element_type=F32)
            carry_ref[h, rows, :] += rs[h][:, CHUNK:]

    first = blk * SB_Q_TILES
    for d in range(SB_Q_TILES):
        j, r0 = first + (SB_Q_TILES - 1) - d, (SB_Q_TILES - 1 - d) * CHUNK
        back(j, r0, True, *front(j, r0, True))

    def stage(j):
        log_betas, rs = front(j, 0, False)
        for h in heads:
            lb_ref[h] = log_betas[h]
            r_ref[h] = rs[h]

    stage(jnp.maximum(first - 1, 0))

    def below_body(jj, c):
        j = first - 1 - jj
        back(j, 0, False, [lb_ref[h] for h in heads], [r_ref[h] for h in heads])
        stage(jnp.maximum(j - 1, 0))
        return c

    lax.fori_loop(0, first, below_body, 0)

    for h in heads:
        y_ref[0, :, sls[h]] = (acc_ref[h] * _silu(z_ref[0, :, sls[h]].astype(F32))).astype(y_ref.dtype)


def _sb(proj, tri):
    bsz, s_len, _ = proj.shape
    off_b = 3 * D_A // D_B

    def row_spec(col):
        return pl.BlockSpec((1, SB_Q_BLOCK, D_B), lambda b, i, c=col: (b, i, c))

    def seq_spec(col):
        return pl.BlockSpec((1, s_len, D_B), lambda b, i, c=col: (b, 0, c))

    return pl.pallas_call(
        _sb_kernel,
        grid=(bsz, s_len // SB_Q_BLOCK),
        in_specs=[
            row_spec(off_b), seq_spec(off_b + 1), seq_spec(off_b + 2), row_spec(off_b + 3),
            pl.BlockSpec((2 * CHUNK, 2 * CHUNK), lambda b, i: (0, 0)),
        ],
        out_specs=pl.BlockSpec((1, SB_Q_BLOCK, D_B), lambda b, i: (b, i, 0)),
        out_shape=jax.ShapeDtypeStruct((bsz, s_len, D_B), BF16),
        scratch_shapes=[pltpu.VMEM((B_HEADS, SB_Q_BLOCK, B_HEAD_DIM), F32),
                        pltpu.VMEM((B_HEADS, SB_Q_BLOCK, B_HEAD_DIM), F32),
                        pltpu.VMEM((B_HEADS, SB_Q_BLOCK, CHUNK), F32),
                        pltpu.VMEM((B_HEADS, SB_Q_BLOCK, 2 * CHUNK), F32)],
        compiler_params=_params(2),
        name="sb",
    )(proj, proj, proj, proj, tri)


def _outproj_kernel(ya_ref, yb_ref, yc_ref, wa_ref, wb_ref, wc_ref, x_ref, o_ref):
    acc = jnp.dot(ya_ref[...], wa_ref[...], preferred_element_type=F32)
    acc += jnp.dot(yb_ref[...], wb_ref[...], preferred_element_type=F32)
    acc += jnp.dot(yc_ref[...], wc_ref[...], preferred_element_type=F32)
    o_ref[...] = x_ref[...] + acc


def _outproj(ya, yb, yc, w_bf16, x2d, *, tm=1024, tn=1024):
    m = x2d.shape[0]
    n = w_bf16.shape[1]
    return pl.pallas_call(
        _outproj_kernel,
        grid=(m // tm, n // tn),
        in_specs=[
            pl.BlockSpec((tm, D_A), lambda i, j: (i, 0)),
            pl.BlockSpec((tm, D_B), lambda i, j: (i, 0)),
            pl.BlockSpec((tm, D_C), lambda i, j: (i, 0)),
            pl.BlockSpec((D_A, tn), lambda i, j: (0, j)),
            pl.BlockSpec((D_B, tn), lambda i, j: (D_A // D_B, j)),
            pl.BlockSpec((D_C, tn), lambda i, j: ((D_A + D_B) // D_C, j)),
            pl.BlockSpec((tm, tn), lambda i, j: (i, j)),
        ],
        out_specs=pl.BlockSpec((tm, tn), lambda i, j: (i, j)),
        out_shape=jax.ShapeDtypeStruct((m, n), F32),
        compiler_params=_params(2),
        name="outproj",
    )(ya, yb, yc, w_bf16, w_bf16, w_bf16, x2d)


def _cumsum_matrix():
    j = lax.broadcasted_iota(jnp.int32, (CHUNK, CHUNK), 0)
    s = lax.broadcasted_iota(jnp.int32, (CHUNK, CHUNK), 1)
    half = jnp.concatenate([(j > s).astype(BF16), jnp.ones((CHUNK, CHUNK), BF16)], axis=1)
    return jnp.concatenate([half, half], axis=0)


def kernel(x, mem, norm_g, w_in, sgu_ln_g, sgu_ln_b, sgu_w, sgu_b, mem_norm_g, w_mem_kv,
           q_norm_g, k_norm_g, w_out):
    bsz, s_len, _ = x.shape
    depth = w_in.shape[0]
    assert s_len % SB_Q_BLOCK == 0 and x.shape[2] == D_MODEL and mem.shape[1] == MEM_LEN
    m = bsz * s_len

    tri = _cumsum_matrix()
    col_scale = jnp.ones((1, IN_WIDTH), F32).at[:, 3 * D_A:3 * D_A + D_B].set(
        LOG2E / math.sqrt(B_HEAD_DIM))
    tril = jnp.tril(jnp.ones((CHUNK, CHUNK), dtype=bool))
    memk, memv = _memkv(mem, mem_norm_g, w_mem_kv.astype(BF16), k_norm_g)

    x2d = x.reshape(m, D_MODEL)
    for l in range(depth):
        proj = _inproj(x2d, norm_g[l].reshape(1, D_MODEL), w_in[l].astype(BF16), col_scale)
        proj = proj.reshape(bsz, s_len, IN_WIDTH)
        w_tril = jnp.where(tril[None], sgu_w[l], 0.0).astype(BF16)
        bias_full = jnp.repeat(sgu_b[l].T, A_GROUP_DIM, axis=1)
        ya, yc = _mixer(proj, memk, memv, l,
                        sgu_ln_g[l].reshape(1, D_A), sgu_ln_b[l].reshape(1, D_A),
                        w_tril, bias_full, q_norm_g[l].reshape(1, C_HEAD_DIM))
        yb = _sb(proj, tri)
        x2d = _outproj(ya.reshape(m, D_A), yb.reshape(m, D_B), yc.reshape(m, D_C),
                       w_out[l].astype(BF16), x2d)
    return x2d.reshape(bsz, s_len, D_MODEL)
```

```python
import math

import jax
import jax.numpy as jnp
from jax import lax
from jax.experimental import pallas as pl
from jax.experimental.pallas import tpu as pltpu

D_MODEL = 2048
MEM_LEN = 256
CHUNK = 128
D_A = D_MODEL // 2
A_GROUPS = 8
A_GROUP_DIM = D_A // A_GROUPS
D_B = D_MODEL // 4
B_HEADS = 4
B_HEAD_DIM = D_B // B_HEADS
D_C = D_MODEL // 4
C_HEADS = 4
C_HEAD_DIM = D_C // C_HEADS
IN_WIDTH = 3 * D_A + 4 * D_B + 2 * D_C
EPS = 1e-6
LOG2E = math.log2(math.e)

SB_Q_TILES = 4
SB_Q_BLOCK = SB_Q_TILES * CHUNK
SB_DEAD_LOG2 = -150.0

F32 = jnp.float32
BF16 = jnp.bfloat16

VMEM_LIMIT_BYTES = 56 * 1024 * 1024

_NT = (((1,), (1,)), ((), ()))


def _gelu(x):
    return 0.5 * x * (1.0 + lax.erf(x * math.sqrt(0.5)))


def _silu(x):
    return x / (1.0 + jnp.exp(-x))


def _params(n_grid_dims):
    return pltpu.CompilerParams(
        dimension_semantics=("arbitrary",) * n_grid_dims,
        vmem_limit_bytes=VMEM_LIMIT_BYTES)


def _inproj_kernel(x_ref, g_ref, w_ref, cs_ref, o_ref, h_ref):
    @pl.when(pl.program_id(1) == 0)
    def _():
        x = x_ref[...]
        ms = jnp.mean(x * x, axis=-1, keepdims=True)
        h_ref[...] = (x * lax.rsqrt(ms + EPS) * g_ref[...]).astype(BF16)

    acc = jnp.dot(h_ref[...], w_ref[...], preferred_element_type=F32)
    o_ref[...] = (acc * cs_ref[...]).astype(o_ref.dtype)


def _inproj(x2d, g, w_bf16, layer, col_scale, *, tm=1024, tn=1024):
    m, k = x2d.shape
    n = w_bf16.shape[2]
    return pl.pallas_call(
        _inproj_kernel,
        grid=(m // tm, n // tn),
        in_specs=[
            pl.BlockSpec((tm, k), lambda i, j: (i, 0)),
            pl.BlockSpec((1, k), lambda i, j: (0, 0)),
            pl.BlockSpec((None, k, tn), lambda i, j: (layer, 0, j)),
            pl.BlockSpec((1, tn), lambda i, j: (0, j)),
        ],
        out_specs=pl.BlockSpec((tm, tn), lambda i, j: (i, j)),
        out_shape=jax.ShapeDtypeStruct((m, n), BF16),
        scratch_shapes=[pltpu.VMEM((tm, k), BF16)],
        compiler_params=_params(2),
        name="inproj",
    )(x2d, g, w_bf16, col_scale)


def _memkv_kernel(mem_ref, g_ref, w_ref, kg_ref, k_ref, v_ref):
    x = mem_ref[0]
    ms = jnp.mean(x * x, axis=-1, keepdims=True)
    h = (x * lax.rsqrt(ms + EPS) * g_ref[0]).astype(BF16)
    kv = jnp.dot(h, w_ref[0], preferred_element_type=F32)
    for hd in range(C_HEADS):
        sl = slice(hd * C_HEAD_DIM, (hd + 1) * C_HEAD_DIM)
        kh = kv[:, sl]
        kms = jnp.mean(kh * kh, axis=-1, keepdims=True)
        k_ref[0, 0, :, sl] = (kh * lax.rsqrt(kms + EPS) * kg_ref[0]).astype(BF16)
    v_ref[0, 0] = kv[:, D_C:].astype(BF16)


def _memkv(mem, mem_norm_g, w_mem_kv_bf16, k_norm_g):
    bsz = mem.shape[0]
    depth = w_mem_kv_bf16.shape[0]
    out = jax.ShapeDtypeStruct((depth, bsz, MEM_LEN, D_C), BF16)
    return pl.pallas_call(
        _memkv_kernel,
        grid=(depth, bsz),
        in_specs=[
            pl.BlockSpec((1, MEM_LEN, D_MODEL), lambda l, b: (b, 0, 0)),
            pl.BlockSpec((1, 1, D_MODEL), lambda l, b: (l, 0, 0)),
            pl.BlockSpec((1, D_MODEL, 2 * D_C), lambda l, b: (l, 0, 0)),
            pl.BlockSpec((1, 1, C_HEAD_DIM), lambda l, b: (l, 0, 0)),
        ],
        out_specs=[
            pl.BlockSpec((1, 1, MEM_LEN, D_C), lambda l, b: (l, b, 0, 0)),
            pl.BlockSpec((1, 1, MEM_LEN, D_C), lambda l, b: (l, b, 0, 0)),
        ],
        out_shape=[out, out],
        compiler_params=_params(2),
        name="memkv",
    )(mem, mem_norm_g.reshape(depth, 1, D_MODEL), w_mem_kv_bf16,
      k_norm_g.reshape(depth, 1, C_HEAD_DIM))


def _mixer_kernel(u_ref, v_ref, za_ref, qc_ref, zc_ref, mk_ref, mv_ref,
                  lng_ref, lnb_ref, w_ref, bias_ref, qg_ref, ya_ref, yc_ref):
    v = _gelu(v_ref[0].astype(F32))
    mu = jnp.mean(v, axis=-1, keepdims=True)
    vc = v - mu
    var = jnp.mean(vc * vc, axis=-1, keepdims=True)
    vn = (vc * lax.rsqrt(var + EPS) * lng_ref[...] + lnb_ref[...]).astype(BF16)
    for g in range(A_GROUPS):
        sl = slice(g * A_GROUP_DIM, (g + 1) * A_GROUP_DIM)
        mixed = jnp.dot(w_ref[g], vn[:, sl], preferred_element_type=F32) + bias_ref[:, sl]
        u = _gelu(u_ref[0, :, sl].astype(F32))
        ya_ref[0, :, sl] = (u * mixed * _silu(za_ref[0, :, sl].astype(F32))).astype(ya_ref.dtype)

    for h in range(C_HEADS):
        sl = slice(h * C_HEAD_DIM, (h + 1) * C_HEAD_DIM)
        q = qc_ref[0, :, sl].astype(F32)
        qms = jnp.mean(q * q, axis=-1, keepdims=True)
        qn = (q * lax.rsqrt(qms + EPS) * qg_ref[...]).astype(BF16)
        s = lax.dot_general(qn, mk_ref[0, 0, :, sl], _NT,
                            preferred_element_type=F32) * (1.0 / math.sqrt(C_HEAD_DIM))
        p = jnp.exp(s - jnp.max(s, axis=-1, keepdims=True))
        denom = jnp.sum(p, axis=-1, keepdims=True)
        o = jnp.dot(p.astype(BF16), mv_ref[0, 0, :, sl], preferred_element_type=F32) / denom
        yc_ref[0, :, sl] = (o * _silu(zc_ref[0, :, sl].astype(F32))).astype(yc_ref.dtype)


def _mixer(proj, memk, memv, layer, lng, lnb, w_tril, bias_full, qg):
    bsz, s_len, _ = proj.shape
    off_b = 3 * D_A // D_B

    def row_spec(width, col):
        return pl.BlockSpec((1, CHUNK, width), lambda b, i, c=col: (b, i, c))

    def const_spec(shape):
        nd = len(shape)
        return pl.BlockSpec(shape, lambda b, i, nd=nd: (0,) * nd)

    mem_spec = pl.BlockSpec((1, 1, MEM_LEN, D_C), lambda b, i, l=layer: (l, b, 0, 0))
    return pl.pallas_call(
        _mixer_kernel,
        grid=(bsz, s_len // CHUNK),
        in_specs=[
            row_spec(D_A, 0), row_spec(D_A, 1), row_spec(D_A, 2),
            row_spec(D_C, off_b + 4), row_spec(D_C, off_b + 5),
            mem_spec, mem_spec,
            const_spec((1, D_A)), const_spec((1, D_A)),
            const_spec((A_GROUPS, CHUNK, CHUNK)), const_spec((CHUNK, D_A)),
            const_spec((1, C_HEAD_DIM)),
        ],
        out_specs=[pl.BlockSpec((1, CHUNK, D_A), lambda b, i: (b, i, 0)),
                   pl.BlockSpec((1, CHUNK, D_C), lambda b, i: (b, i, 0))],
        out_shape=[jax.ShapeDtypeStruct((bsz, s_len, D_A), BF16),
                   jax.ShapeDtypeStruct((bsz, s_len, D_C), BF16)],
        compiler_params=_params(2),
        name="mixer",
    )(proj, proj, proj, proj, proj, memk, memv, lng, lnb, w_tril, bias_full, qg)


def _sb_kernel(q_ref, k_ref, v_ref, z_ref, tri_ref, y_ref, carry_ref, acc_ref, lb_ref, r_ref):
    blk = pl.program_id(1)
    carry_ref[...] = jnp.zeros_like(carry_ref)
    acc_ref[...] = jnp.zeros_like(acc_ref)
    row = lax.broadcasted_iota(jnp.int32, (CHUNK, CHUNK), 0)
    col = lax.broadcasted_iota(jnp.int32, (CHUNK, CHUNK), 1)
    causal = col < row
    heads = range(B_HEADS)
    sls = [slice(h * B_HEAD_DIM, (h + 1) * B_HEAD_DIM) for h in heads]

    def mask_top(x, r0, diag):
        if not diag:
            return x
        top = jnp.where(causal, x[:CHUNK], 0.0)
        return top if r0 + CHUNK == SB_Q_BLOCK else jnp.concatenate([top, x[CHUNK:]], axis=0)

    def front(j, r0, diag):
        start = pl.multiple_of(j * CHUNK, CHUNK)
        rows = slice(r0, SB_Q_BLOCK)
        zs = [lax.dot_general(q_ref[0, rows, sls[h]], k_ref[0, pl.ds(start, CHUNK), sls[h]], _NT,
                              preferred_element_type=F32) for h in heads]
        log_betas, hls = [], []
        for h in heads:
            z = zs[h]
            e = jnp.log(1.0 + jnp.exp2(-jnp.abs(z))) * LOG2E
            log_beta = jnp.minimum(z, 0.0) - e
            log_1mb = mask_top(log_beta - z, r0, diag)
            hi = log_1mb.astype(BF16)
            lo = (log_1mb - hi.astype(F32)).astype(BF16)
            log_betas.append(log_beta)
            hls.append(jnp.concatenate([hi, lo], axis=1))
        rs = [jnp.dot(hls[h], tri_ref[...], preferred_element_type=F32) for h in heads]
        return log_betas, rs

    def back(j, r0, diag, log_betas, rs):
        start = pl.multiple_of(j * CHUNK, CHUNK)
        rows = slice(r0, SB_Q_BLOCK)
        for h in heads:
            a = mask_top(jnp.exp2(log_betas[h] + rs[h][:, :CHUNK] + carry_ref[h, rows, :]), r0, diag)
            acc_ref[h, rows, :] += jnp.dot(a.astype(BF16), v_ref[0, pl.ds(start, CHUNK), sls[h]],
                                           preferred_element_type=F32)
            carry_ref[h, rows, :] += rs[h][:, CHUNK:]

    first = blk * SB_Q_TILES
    for d in range(SB_Q_TILES):
        j, r0 = first + (SB_Q_TILES - 1) - d, (SB_Q_TILES - 1 - d) * CHUNK
        back(j, r0, True, *front(j, r0, True))

    def stage(j):
        log_betas, rs = front(j, 0, False)
        for h in heads:
            lb_ref[h] = log_betas[h]
            r_ref[h] = rs[h]

    stage(jnp.maximum(first - 1, 0))

    def live():
        return (jnp.max(carry_ref[...]) > SB_DEAD_LOG2).astype(jnp.int32)

    def below_cond(c):
        jj, alive = c
        return jnp.logical_and(jj < first, alive > 0)

    def below_body(c):
        jj, _ = c
        j = first - 1 - jj
        back(j, 0, False, [lb_ref[h] for h in heads], [r_ref[h] for h in heads])
        stage(jnp.maximum(j - 1, 0))
        return jj + 1, live()

    lax.while_loop(below_cond, below_body, (jnp.int32(0), live()))

    for h in heads:
        y_ref[0, :, sls[h]] = (acc_ref[h] * _silu(z_ref[0, :, sls[h]].astype(F32))).astype(y_ref.dtype)


def _sb(proj, tri):
    bsz, s_len, _ = proj.shape
    off_b = 3 * D_A // D_B

    def row_spec(col):
        return pl.BlockSpec((1, SB_Q_BLOCK, D_B), lambda b, i, c=col: (b, i, c))

    def seq_spec(col):
        return pl.BlockSpec((1, s_len, D_B), lambda b, i, c=col: (b, 0, c))

    return pl.pallas_call(
        _sb_kernel,
        grid=(bsz, s_len // SB_Q_BLOCK),
        in_specs=[
            row_spec(off_b), seq_spec(off_b + 1), seq_spec(off_b + 2), row_spec(off_b + 3),
            pl.BlockSpec((2 * CHUNK, 2 * CHUNK), lambda b, i: (0, 0)),
        ],
        out_specs=pl.BlockSpec((1, SB_Q_BLOCK, D_B), lambda b, i: (b, i, 0)),
        out_shape=jax.ShapeDtypeStruct((bsz, s_len, D_B), BF16),
        scratch_shapes=[pltpu.VMEM((B_HEADS, SB_Q_BLOCK, B_HEAD_DIM), F32),
                        pltpu.VMEM((B_HEADS, SB_Q_BLOCK, B_HEAD_DIM), F32),
                        pltpu.VMEM((B_HEADS, SB_Q_BLOCK, CHUNK), F32),
                        pltpu.VMEM((B_HEADS, SB_Q_BLOCK, 2 * CHUNK), F32)],
        compiler_params=_params(2),
        name="sb",
    )(proj, proj, proj, proj, tri)


def _outproj_kernel(ya_ref, yb_ref, yc_ref, wa_ref, wb_ref, wc_ref, x_ref, o_ref):
    acc = jnp.dot(ya_ref[...], wa_ref[...], preferred_element_type=F32)
    acc += jnp.dot(yb_ref[...], wb_ref[...], preferred_element_type=F32)
    acc += jnp.dot(yc_ref[...], wc_ref[...], preferred_element_type=F32)
    o_ref[...] = x_ref[...] + acc


def _outproj(ya, yb, yc, w_bf16, layer, x2d, *, tm=1024, tn=1024):
    m = x2d.shape[0]
    n = w_bf16.shape[2]
    return pl.pallas_call(
        _outproj_kernel,
        grid=(m // tm, n // tn),
        in_specs=[
            pl.BlockSpec((tm, D_A), lambda i, j: (i, 0)),
            pl.BlockSpec((tm, D_B), lambda i, j: (i, 0)),
            pl.BlockSpec((tm, D_C), lambda i, j: (i, 0)),
            pl.BlockSpec((None, D_A, tn), lambda i, j: (layer, 0, j)),
            pl.BlockSpec((None, D_B, tn), lambda i, j: (layer, D_A // D_B, j)),
            pl.BlockSpec((None, D_C, tn), lambda i, j: (layer, (D_A + D_B) // D_C, j)),
            pl.BlockSpec((tm, tn), lambda i, j: (i, j)),
        ],
        out_specs=pl.BlockSpec((tm, tn), lambda i, j: (i, j)),
        out_shape=jax.ShapeDtypeStruct((m, n), F32),
        compiler_params=_params(2),
        name="outproj",
    )(ya, yb, yc, w_bf16, w_bf16, w_bf16, x2d)


def _cumsum_matrix():
    j = lax.broadcasted_iota(jnp.int32, (CHUNK, CHUNK), 0)
    s = lax.broadcasted_iota(jnp.int32, (CHUNK, CHUNK), 1)
    half = jnp.concatenate([(j > s).astype(BF16), jnp.ones((CHUNK, CHUNK), BF16)], axis=1)
    return jnp.concatenate([half, half], axis=0)


def kernel(x, mem, norm_g, w_in, sgu_ln_g, sgu_ln_b, sgu_w, sgu_b, mem_norm_g, w_mem_kv,
           q_norm_g, k_norm_g, w_out):
    bsz, s_len, _ = x.shape
    depth = w_in.shape[0]
    assert s_len % SB_Q_BLOCK == 0 and x.shape[2] == D_MODEL and mem.shape[1] == MEM_LEN
    m = bsz * s_len

    tri = _cumsum_matrix()
    col_scale = jnp.ones((1, IN_WIDTH), F32).at[:, 3 * D_A:3 * D_A + D_B].set(
        LOG2E / math.sqrt(B_HEAD_DIM))
    tril = jnp.tril(jnp.ones((CHUNK, CHUNK), dtype=bool))
    memk, memv = _memkv(mem, mem_norm_g, w_mem_kv.astype(BF16), k_norm_g)
    w_in_bf16 = w_in.astype(BF16)
    w_out_bf16 = w_out.astype(BF16)

    x2d = x.reshape(m, D_MODEL)
    for l in range(depth):
        proj = _inproj(x2d, norm_g[l].reshape(1, D_MODEL), w_in_bf16, l, col_scale)
        proj = proj.reshape(bsz, s_len, IN_WIDTH)
        w_tril = jnp.where(tril[None], sgu_w[l], 0.0).astype(BF16)
        bias_full = jnp.repeat(sgu_b[l].T, A_GROUP_DIM, axis=1)
        ya, yc = _mixer(proj, memk, memv, l,
                        sgu_ln_g[l].reshape(1, D_A), sgu_ln_b[l].reshape(1, D_A),
                        w_tril, bias_full, q_norm_g[l].reshape(1, C_HEAD_DIM))
        yb = _sb(proj, tri)
        x2d = _outproj(ya.reshape(m, D_A), yb.reshape(m, D_B), yc.reshape(m, D_C),
                       w_out_bf16, l, x2d)
    return x2d.reshape(bsz, s_len, D_MODEL)
```

```python
import math

import jax
import jax.numpy as jnp
from jax import lax
from jax.experimental import pallas as pl
from jax.experimental.pallas import tpu as pltpu

D_MODEL = 2048
MEM_LEN = 256
CHUNK = 128
D_A = D_MODEL // 2
A_GROUPS = 8
A_GROUP_DIM = D_A // A_GROUPS
D_B = D_MODEL // 4
B_HEADS = 4
B_HEAD_DIM = D_B // B_HEADS
D_C = D_MODEL // 4
C_HEADS = 4
C_HEAD_DIM = D_C // C_HEADS
IN_WIDTH = 3 * D_A + 4 * D_B + 2 * D_C
EPS = 1e-6
LOG2E = math.log2(math.e)

SB_Q_TILES = 4
SB_Q_BLOCK = SB_Q_TILES * CHUNK
SB_DEAD_LOG2 = -150.0
MO_ROWS = 512
MO_PIECE = 128

F32 = jnp.float32
BF16 = jnp.bfloat16

VMEM_LIMIT_BYTES = 56 * 1024 * 1024

_NT = (((1,), (1,)), ((), ()))


def _gelu(x):
    return 0.5 * x * (1.0 + lax.erf(x * math.sqrt(0.5)))


def _silu(x):
    return x / (1.0 + jnp.exp(-x))


def _params(n_grid_dims):
    return pltpu.CompilerParams(
        dimension_semantics=("arbitrary",) * n_grid_dims,
        vmem_limit_bytes=VMEM_LIMIT_BYTES)


def _inproj_kernel(x_ref, g_ref, w_ref, cs_ref, o_ref, h_ref):
    @pl.when(pl.program_id(1) == 0)
    def _():
        x = x_ref[...]
        ms = jnp.mean(x * x, axis=-1, keepdims=True)
        h_ref[...] = (x * lax.rsqrt(ms + EPS) * g_ref[...]).astype(BF16)

    acc = jnp.dot(h_ref[...], w_ref[...], preferred_element_type=F32)
    o_ref[...] = (acc * cs_ref[...]).astype(o_ref.dtype)


def _inproj(x2d, g, w_bf16, layer, col_scale, *, tm=1024, tn=1024):
    m, k = x2d.shape
    n = w_bf16.shape[2]
    return pl.pallas_call(
        _inproj_kernel,
        grid=(m // tm, n // tn),
        in_specs=[
            pl.BlockSpec((tm, k), lambda i, j: (i, 0)),
            pl.BlockSpec((1, k), lambda i, j: (0, 0)),
            pl.BlockSpec((None, k, tn), lambda i, j: (layer, 0, j)),
            pl.BlockSpec((1, tn), lambda i, j: (0, j)),
        ],
        out_specs=pl.BlockSpec((tm, tn), lambda i, j: (i, j)),
        out_shape=jax.ShapeDtypeStruct((m, n), BF16),
        scratch_shapes=[pltpu.VMEM((tm, k), BF16)],
        compiler_params=_params(2),
        name="inproj",
    )(x2d, g, w_bf16, col_scale)


def _memkv_kernel(mem_ref, g_ref, w_ref, kg_ref, k_ref, v_ref):
    x = mem_ref[0]
    ms = jnp.mean(x * x, axis=-1, keepdims=True)
    h = (x * lax.rsqrt(ms + EPS) * g_ref[0]).astype(BF16)
    kv = jnp.dot(h, w_ref[0], preferred_element_type=F32)
    for hd in range(C_HEADS):
        sl = slice(hd * C_HEAD_DIM, (hd + 1) * C_HEAD_DIM)
        kh = kv[:, sl]
        kms = jnp.mean(kh * kh, axis=-1, keepdims=True)
        k_ref[0, 0, :, sl] = (kh * lax.rsqrt(kms + EPS) * kg_ref[0]).astype(BF16)
    v_ref[0, 0] = kv[:, D_C:].astype(BF16)


def _memkv(mem, mem_norm_g, w_mem_kv_bf16, k_norm_g):
    bsz = mem.shape[0]
    depth = w_mem_kv_bf16.shape[0]
    out = jax.ShapeDtypeStruct((depth, bsz, MEM_LEN, D_C), BF16)
    return pl.pallas_call(
        _memkv_kernel,
        grid=(depth, bsz),
        in_specs=[
            pl.BlockSpec((1, MEM_LEN, D_MODEL), lambda l, b: (b, 0, 0)),
            pl.BlockSpec((1, 1, D_MODEL), lambda l, b: (l, 0, 0)),
            pl.BlockSpec((1, D_MODEL, 2 * D_C), lambda l, b: (l, 0, 0)),
            pl.BlockSpec((1, 1, C_HEAD_DIM), lambda l, b: (l, 0, 0)),
        ],
        out_specs=[
            pl.BlockSpec((1, 1, MEM_LEN, D_C), lambda l, b: (l, b, 0, 0)),
            pl.BlockSpec((1, 1, MEM_LEN, D_C), lambda l, b: (l, b, 0, 0)),
        ],
        out_shape=[out, out],
        compiler_params=_params(2),
        name="memkv",
    )(mem, mem_norm_g.reshape(depth, 1, D_MODEL), w_mem_kv_bf16,
      k_norm_g.reshape(depth, 1, C_HEAD_DIM))


def _mixout_kernel(u_ref, v_ref, za_ref, qc_ref, zc_ref, yb_ref, mk_ref, mv_ref,
                   lng_ref, lnb_ref, w_ref, bias_ref, qg_ref, w_out_ref, x_ref, o_ref):
    def mix(c):
        rows = slice(c * CHUNK, (c + 1) * CHUNK)
        v = _gelu(v_ref[rows, :].astype(F32))
        mu = jnp.mean(v, axis=-1, keepdims=True)
        vc = v - mu
        var = jnp.mean(vc * vc, axis=-1, keepdims=True)
        vn = (vc * lax.rsqrt(var + EPS) * lng_ref[...] + lnb_ref[...]).astype(BF16)
        ya = []
        for g in range(A_GROUPS):
            sl = slice(g * A_GROUP_DIM, (g + 1) * A_GROUP_DIM)
            mixed = jnp.dot(w_ref[g], vn[:, sl], preferred_element_type=F32) + bias_ref[:, sl]
            u = _gelu(u_ref[rows, sl].astype(F32))
            ya.append((u * mixed * _silu(za_ref[rows, sl].astype(F32))).astype(BF16))
        yc = []
        for h in range(C_HEADS):
            sl = slice(h * C_HEAD_DIM, (h + 1) * C_HEAD_DIM)
            q = qc_ref[rows, sl].astype(F32)
            qms = jnp.mean(q * q, axis=-1, keepdims=True)
            qn = (q * lax.rsqrt(qms + EPS) * qg_ref[...]).astype(BF16)
            s = lax.dot_general(qn, mk_ref[:, sl], _NT,
                                preferred_element_type=F32) * (1.0 / math.sqrt(C_HEAD_DIM))
            p = jnp.exp(s - jnp.max(s, axis=-1, keepdims=True))
            denom = jnp.sum(p, axis=-1, keepdims=True)
            o = jnp.dot(p.astype(BF16), mv_ref[:, sl], preferred_element_type=F32) / denom
            yc.append((o * _silu(zc_ref[rows, sl].astype(F32))).astype(BF16))
        return jnp.concatenate(ya + [yb_ref[rows, :]] + yc, axis=1)

    def project(p, ys):
        rows = slice(p * MO_PIECE, (p + 1) * MO_PIECE)
        y = jnp.concatenate(ys, axis=0)
        o_ref[rows, :] = x_ref[rows, :] + jnp.dot(y, w_out_ref[...], preferred_element_type=F32)

    per_piece = MO_PIECE // CHUNK
    pieces = MO_ROWS // MO_PIECE
    mix_piece = lambda p: [mix(p * per_piece + c) for c in range(per_piece)]
    ready = mix_piece(0)
    for p in range(pieces):
        nxt = mix_piece(p + 1) if p + 1 < pieces else None
        project(p, ready)
        ready = nxt


def _mixout(proj2d, yb2d, memk, memv, layer, s_len, lng, lnb, w_tril, bias_full, qg,
            w_out_bf16, x2d):
    m = x2d.shape[0]
    off_b = 3 * D_A // D_B
    blocks_per_seq = s_len // MO_ROWS

    def row_spec(width, col):
        return pl.BlockSpec((MO_ROWS, width), lambda i, c=col: (i, c))

    def const_spec(shape):
        nd = len(shape)
        return pl.BlockSpec(shape, lambda i, nd=nd: (0,) * nd)

    mem_spec = pl.BlockSpec((None, None, MEM_LEN, D_C),
                            lambda i: (layer, i // blocks_per_seq, 0, 0))
    return pl.pallas_call(
        _mixout_kernel,
        grid=(m // MO_ROWS,),
        in_specs=[
            row_spec(D_A, 0), row_spec(D_A, 1), row_spec(D_A, 2),
            row_spec(D_C, off_b + 4), row_spec(D_C, off_b + 5),
            row_spec(D_B, 0),
            mem_spec, mem_spec,
            const_spec((1, D_A)), const_spec((1, D_A)),
            const_spec((A_GROUPS, CHUNK, CHUNK)), const_spec((CHUNK, D_A)),
            const_spec((1, C_HEAD_DIM)),
            pl.BlockSpec((None, D_MODEL, D_MODEL), lambda i: (layer, 0, 0)),
            row_spec(D_MODEL, 0),
        ],
        out_specs=pl.BlockSpec((MO_ROWS, D_MODEL), lambda i: (i, 0)),
        out_shape=jax.ShapeDtypeStruct((m, D_MODEL), F32),
        compiler_params=_params(1),
        name="mixout",
    )(proj2d, proj2d, proj2d, proj2d, proj2d, yb2d, memk, memv, lng, lnb, w_tril, bias_full, qg,
      w_out_bf16, x2d)


def _sb_kernel(q_ref, k_ref, v_ref, z_ref, tri_ref, y_ref, carry_ref, acc_ref, lb_ref, r_ref):
    blk = pl.program_id(1)
    carry_ref[...] = jnp.zeros_like(carry_ref)
    acc_ref[...] = jnp.zeros_like(acc_ref)
    row = lax.broadcasted_iota(jnp.int32, (CHUNK, CHUNK), 0)
    col = lax.broadcasted_iota(jnp.int32, (CHUNK, CHUNK), 1)
    causal = col < row
    heads = range(B_HEADS)
    sls = [slice(h * B_HEAD_DIM, (h + 1) * B_HEAD_DIM) for h in heads]

    def mask_top(x, r0, diag):
        if not diag:
            return x
        top = jnp.where(causal, x[:CHUNK], 0.0)
        return top if r0 + CHUNK == SB_Q_BLOCK else jnp.concatenate([top, x[CHUNK:]], axis=0)

    def front(j, r0, diag):
        start = pl.multiple_of(j * CHUNK, CHUNK)
        rows = slice(r0, SB_Q_BLOCK)
        zs = [lax.dot_general(q_ref[0, rows, sls[h]], k_ref[0, pl.ds(start, CHUNK), sls[h]], _NT,
                              preferred_element_type=F32) for h in heads]
        log_betas, hls = [], []
        for h in heads:
            z = zs[h]
            e = jnp.log(1.0 + jnp.exp2(-jnp.abs(z))) * LOG2E
            log_beta = jnp.minimum(z, 0.0) - e
            log_1mb = mask_top(log_beta - z, r0, diag)
            hi = log_1mb.astype(BF16)
            lo = (log_1mb - hi.astype(F32)).astype(BF16)
            log_betas.append(log_beta)
            hls.append(jnp.concatenate([hi, lo], axis=1))
        rs = [jnp.dot(hls[h], tri_ref[...], preferred_element_type=F32) for h in heads]
        return log_betas, rs

    def back(j, r0, diag, log_betas, rs):
        start = pl.multiple_of(j * CHUNK, CHUNK)
        rows = slice(r0, SB_Q_BLOCK)
        for h in heads:
            a = mask_top(jnp.exp2(log_betas[h] + rs[h][:, :CHUNK] + carry_ref[h, rows, :]), r0, diag)
            acc_ref[h, rows, :] += jnp.dot(a.astype(BF16), v_ref[0, pl.ds(start, CHUNK), sls[h]],
                                           preferred_element_type=F32)
            carry_ref[h, rows, :] += rs[h][:, CHUNK:]

    first = blk * SB_Q_TILES
    for d in range(SB_Q_TILES):
        j, r0 = first + (SB_Q_TILES - 1) - d, (SB_Q_TILES - 1 - d) * CHUNK
        back(j, r0, True, *front(j, r0, True))

    def stage(j):
        log_betas, rs = front(j, 0, False)
        for h in heads:
            lb_ref[h] = log_betas[h]
            r_ref[h] = rs[h]

    stage(jnp.maximum(first - 1, 0))

    def live():
        return (jnp.max(carry_ref[...]) > SB_DEAD_LOG2).astype(jnp.int32)

    def below_cond(c):
        jj, alive = c
        return jnp.logical_and(jj < first, alive > 0)

    def below_body(c):
        jj, _ = c
        j = first - 1 - jj
        back(j, 0, False, [lb_ref[h] for h in heads], [r_ref[h] for h in heads])
        stage(jnp.maximum(j - 1, 0))
        return jj + 1, live()

    lax.while_loop(below_cond, below_body, (jnp.int32(0), live()))

    for h in heads:
        y_ref[0, :, sls[h]] = (acc_ref[h] * _silu(z_ref[0, :, sls[h]].astype(F32))).astype(y_ref.dtype)


def _sb(proj, tri):
    bsz, s_len, _ = proj.shape
    off_b = 3 * D_A // D_B

    def row_spec(col):
        return pl.BlockSpec((1, SB_Q_BLOCK, D_B), lambda b, i, c=col: (b, i, c))

    def seq_spec(col):
        return pl.BlockSpec((1, s_len, D_B), lambda b, i, c=col: (b, 0, c))

    return pl.pallas_call(
        _sb_kernel,
        grid=(bsz, s_len // SB_Q_BLOCK),
        in_specs=[
            row_spec(off_b), seq_spec(off_b + 1), seq_spec(off_b + 2), row_spec(off_b + 3),
            pl.BlockSpec((2 * CHUNK, 2 * CHUNK), lambda b, i: (0, 0)),
        ],
        out_specs=pl.BlockSpec((1, SB_Q_BLOCK, D_B), lambda b, i: (b, i, 0)),
        out_shape=jax.ShapeDtypeStruct((bsz, s_len, D_B), BF16),
        scratch_shapes=[pltpu.VMEM((B_HEADS, SB_Q_BLOCK, B_HEAD_DIM), F32),
                        pltpu.VMEM((B_HEADS, SB_Q_BLOCK, B_HEAD_DIM), F32),
                        pltpu.VMEM((B_HEADS, SB_Q_BLOCK, CHUNK), F32),
                        pltpu.VMEM((B_HEADS, SB_Q_BLOCK, 2 * CHUNK), F32)],
        compiler_params=_params(2),
        name="sb",
    )(proj, proj, proj, proj, tri)


def _cumsum_matrix():
    j = lax.broadcasted_iota(jnp.int32, (CHUNK, CHUNK), 0)
    s = lax.broadcasted_iota(jnp.int32, (CHUNK, CHUNK), 1)
    half = jnp.concatenate([(j > s).astype(BF16), jnp.ones((CHUNK, CHUNK), BF16)], axis=1)
    return jnp.concatenate([half, half], axis=0)


def kernel(x, mem, norm_g, w_in, sgu_ln_g, sgu_ln_b, sgu_w, sgu_b, mem_norm_g, w_mem_kv,
           q_norm_g, k_norm_g, w_out):
    bsz, s_len, _ = x.shape
    depth = w_in.shape[0]
    assert s_len % SB_Q_BLOCK == 0 and x.shape[2] == D_MODEL and mem.shape[1] == MEM_LEN
    m = bsz * s_len

    tri = _cumsum_matrix()
    col_scale = jnp.ones((1, IN_WIDTH), F32).at[:, 3 * D_A:3 * D_A + D_B].set(
        LOG2E / math.sqrt(B_HEAD_DIM))
    tril = jnp.tril(jnp.ones((CHUNK, CHUNK), dtype=bool))
    memk, memv = _memkv(mem, mem_norm_g, w_mem_kv.astype(BF16), k_norm_g)
    w_in_bf16 = w_in.astype(BF16)
    w_out_bf16 = w_out.astype(BF16)

    x2d = x.reshape(m, D_MODEL)
    for l in range(depth):
        proj = _inproj(x2d, norm_g[l].reshape(1, D_MODEL), w_in_bf16, l, col_scale)
        w_tril = jnp.where(tril[None], sgu_w[l], 0.0).astype(BF16)
        bias_full = jnp.repeat(sgu_b[l].T, A_GROUP_DIM, axis=1)
        yb = _sb(proj.reshape(bsz, s_len, IN_WIDTH), tri)
        x2d = _mixout(proj, yb.reshape(m, D_B), memk, memv, l, s_len,
                      sgu_ln_g[l].reshape(1, D_A), sgu_ln_b[l].reshape(1, D_A),
                      w_tril, bias_full, q_norm_g[l].reshape(1, C_HEAD_DIM), w_out_bf16, x2d)
    return x2d.reshape(bsz, s_len, D_MODEL)
```

```python
import functools
import math

import jax
import jax.numpy as jnp
from jax import lax
from jax.experimental import pallas as pl
from jax.experimental.pallas import tpu as pltpu

D_MODEL = 2048
MEM_LEN = 256
CHUNK = 128
D_A = D_MODEL // 2
A_GROUPS = 8
A_GROUP_DIM = D_A // A_GROUPS
D_B = D_MODEL // 4
B_HEADS = 4
B_HEAD_DIM = D_B // B_HEADS
D_C = D_MODEL // 4
C_HEADS = 4
C_HEAD_DIM = D_C // C_HEADS
IN_WIDTH = 3 * D_A + 4 * D_B + 2 * D_C
EPS = 1e-6
LOG2E = math.log2(math.e)

SB_Q_TILES = 4
SB_Q_BLOCK = SB_Q_TILES * CHUNK
SB_HALF = SB_Q_BLOCK // 2
SB_DEAD_LOG2 = -150.0
MO_ROWS = 512
MO_PIECE = 128

F32 = jnp.float32
BF16 = jnp.bfloat16

VMEM_LIMIT_BYTES = 56 * 1024 * 1024

_NT = (((1,), (1,)), ((), ()))


def _gelu(x):
    return 0.5 * x * (1.0 + lax.erf(x * math.sqrt(0.5)))


def _silu(x):
    return x / (1.0 + jnp.exp(-x))


def _params(n_grid_dims):
    return pltpu.CompilerParams(
        dimension_semantics=("arbitrary",) * n_grid_dims,
        vmem_limit_bytes=VMEM_LIMIT_BYTES)


def _inproj_kernel(x_ref, g_ref, w_ref, cs_ref, o_ref, h_ref):
    @pl.when(pl.program_id(1) == 0)
    def _():
        x = x_ref[...]
        ms = jnp.mean(x * x, axis=-1, keepdims=True)
        h_ref[...] = (x * lax.rsqrt(ms + EPS) * g_ref[...]).astype(BF16)

    acc = jnp.dot(h_ref[...], w_ref[...], preferred_element_type=F32)
    o_ref[...] = (acc * cs_ref[...]).astype(o_ref.dtype)


def _inproj(x2d, g, w_bf16, layer, col_scale, *, tm=1024, tn=1024):
    m, k = x2d.shape
    n = w_bf16.shape[2]
    return pl.pallas_call(
        _inproj_kernel,
        grid=(m // tm, n // tn),
        in_specs=[
            pl.BlockSpec((tm, k), lambda i, j: (i, 0)),
            pl.BlockSpec((1, k), lambda i, j: (0, 0)),
            pl.BlockSpec((None, k, tn), lambda i, j: (layer, 0, j)),
            pl.BlockSpec((1, tn), lambda i, j: (0, j)),
        ],
        out_specs=pl.BlockSpec((tm, tn), lambda i, j: (i, j)),
        out_shape=jax.ShapeDtypeStruct((m, n), BF16),
        scratch_shapes=[pltpu.VMEM((tm, k), BF16)],
        compiler_params=_params(2),
        name="inproj",
    )(x2d, g, w_bf16, col_scale)


def _memkv_kernel(mem_ref, g_ref, w_ref, kg_ref, k_ref, v_ref):
    x = mem_ref[0]
    ms = jnp.mean(x * x, axis=-1, keepdims=True)
    h = (x * lax.rsqrt(ms + EPS) * g_ref[0]).astype(BF16)
    kv = jnp.dot(h, w_ref[0], preferred_element_type=F32)
    for hd in range(C_HEADS):
        sl = slice(hd * C_HEAD_DIM, (hd + 1) * C_HEAD_DIM)
        kh = kv[:, sl]
        kms = jnp.mean(kh * kh, axis=-1, keepdims=True)
        k_ref[0, 0, :, sl] = (kh * lax.rsqrt(kms + EPS) * kg_ref[0]).astype(BF16)
    v_ref[0, 0] = kv[:, D_C:].astype(BF16)


def _memkv(mem, mem_norm_g, w_mem_kv_bf16, k_norm_g):
    bsz = mem.shape[0]
    depth = w_mem_kv_bf16.shape[0]
    out = jax.ShapeDtypeStruct((depth, bsz, MEM_LEN, D_C), BF16)
    return pl.pallas_call(
        _memkv_kernel,
        grid=(depth, bsz),
        in_specs=[
            pl.BlockSpec((1, MEM_LEN, D_MODEL), lambda l, b: (b, 0, 0)),
            pl.BlockSpec((1, 1, D_MODEL), lambda l, b: (l, 0, 0)),
            pl.BlockSpec((1, D_MODEL, 2 * D_C), lambda l, b: (l, 0, 0)),
            pl.BlockSpec((1, 1, C_HEAD_DIM), lambda l, b: (l, 0, 0)),
        ],
        out_specs=[
            pl.BlockSpec((1, 1, MEM_LEN, D_C), lambda l, b: (l, b, 0, 0)),
            pl.BlockSpec((1, 1, MEM_LEN, D_C), lambda l, b: (l, b, 0, 0)),
        ],
        out_shape=[out, out],
        compiler_params=_params(2),
        name="memkv",
    )(mem, mem_norm_g.reshape(depth, 1, D_MODEL), w_mem_kv_bf16,
      k_norm_g.reshape(depth, 1, C_HEAD_DIM))


def _mixout_kernel(u_ref, v_ref, za_ref, qc_ref, zc_ref, yb_ref, mk_ref, mv_ref,
                   lng_ref, lnb_ref, w_ref, bias_ref, qg_ref, w_out_ref, x_ref, o_ref):
    def mix(c):
        rows = slice(c * CHUNK, (c + 1) * CHUNK)
        v = _gelu(v_ref[rows, :].astype(F32))
        mu = jnp.mean(v, axis=-1, keepdims=True)
        vc = v - mu
        var = jnp.mean(vc * vc, axis=-1, keepdims=True)
        vn = (vc * lax.rsqrt(var + EPS) * lng_ref[...] + lnb_ref[...]).astype(BF16)
        ya = []
        for g in range(A_GROUPS):
            sl = slice(g * A_GROUP_DIM, (g + 1) * A_GROUP_DIM)
            mixed = jnp.dot(w_ref[g], vn[:, sl], preferred_element_type=F32) + bias_ref[:, sl]
            u = _gelu(u_ref[rows, sl].astype(F32))
            ya.append((u * mixed * _silu(za_ref[rows, sl].astype(F32))).astype(BF16))
        yc = []
        for h in range(C_HEADS):
            sl = slice(h * C_HEAD_DIM, (h + 1) * C_HEAD_DIM)
            q = qc_ref[rows, sl].astype(F32)
            qms = jnp.mean(q * q, axis=-1, keepdims=True)
            qn = (q * lax.rsqrt(qms + EPS) * qg_ref[...]).astype(BF16)
            s = lax.dot_general(qn, mk_ref[:, sl], _NT,
                                preferred_element_type=F32) * (1.0 / math.sqrt(C_HEAD_DIM))
            p = jnp.exp(s - jnp.max(s, axis=-1, keepdims=True))
            denom = jnp.sum(p, axis=-1, keepdims=True)
            o = jnp.dot(p.astype(BF16), mv_ref[:, sl], preferred_element_type=F32) / denom
            yc.append((o * _silu(zc_ref[rows, sl].astype(F32))).astype(BF16))
        return jnp.concatenate(ya + [yb_ref[rows, :]] + yc, axis=1)

    def project(p, ys):
        rows = slice(p * MO_PIECE, (p + 1) * MO_PIECE)
        y = jnp.concatenate(ys, axis=0)
        o_ref[rows, :] = x_ref[rows, :] + jnp.dot(y, w_out_ref[...], preferred_element_type=F32)

    per_piece = MO_PIECE // CHUNK
    pieces = MO_ROWS // MO_PIECE
    mix_piece = lambda p: [mix(p * per_piece + c) for c in range(per_piece)]
    ready = mix_piece(0)
    for p in range(pieces):
        nxt = mix_piece(p + 1) if p + 1 < pieces else None
        project(p, ready)
        ready = nxt


def _mixout(proj2d, yb2d, memk, memv, layer, s_len, lng, lnb, w_tril, bias_full, qg,
            w_out_bf16, x2d):
    m = x2d.shape[0]
    off_b = 3 * D_A // D_B
    blocks_per_seq = s_len // MO_ROWS

    def row_spec(width, col):
        return pl.BlockSpec((MO_ROWS, width), lambda i, c=col: (i, c))

    def const_spec(shape):
        nd = len(shape)
        return pl.BlockSpec(shape, lambda i, nd=nd: (0,) * nd)

    mem_spec = pl.BlockSpec((None, None, MEM_LEN, D_C),
                            lambda i: (layer, i // blocks_per_seq, 0, 0))
    return pl.pallas_call(
        _mixout_kernel,
        grid=(m // MO_ROWS,),
        in_specs=[
            row_spec(D_A, 0), row_spec(D_A, 1), row_spec(D_A, 2),
            row_spec(D_C, off_b + 4), row_spec(D_C, off_b + 5),
            row_spec(D_B, 0),
            mem_spec, mem_spec,
            const_spec((1, D_A)), const_spec((1, D_A)),
            const_spec((A_GROUPS, CHUNK, CHUNK)), const_spec((CHUNK, D_A)),
            const_spec((1, C_HEAD_DIM)),
            pl.BlockSpec((None, D_MODEL, D_MODEL), lambda i: (layer, 0, 0)),
            row_spec(D_MODEL, 0),
        ],
        out_specs=pl.BlockSpec((MO_ROWS, D_MODEL), lambda i: (i, 0)),
        out_shape=jax.ShapeDtypeStruct((m, D_MODEL), F32),
        compiler_params=_params(1),
        name="mixout",
    )(proj2d, proj2d, proj2d, proj2d, proj2d, yb2d, memk, memv, lng, lnb, w_tril, bias_full, qg,
      w_out_bf16, x2d)


def _sb_kernel(q_ref, k_ref, v_ref, z_ref, tri_ref, y_ref, carry_ref, acc_ref, lb_ref, r_ref):
    blk = pl.program_id(1)
    carry_ref[...] = jnp.zeros_like(carry_ref)
    acc_ref[...] = jnp.zeros_like(acc_ref)
    row = lax.broadcasted_iota(jnp.int32, (CHUNK, CHUNK), 0)
    col = lax.broadcasted_iota(jnp.int32, (CHUNK, CHUNK), 1)
    causal = col < row
    heads = range(B_HEADS)
    sls = [slice(h * B_HEAD_DIM, (h + 1) * B_HEAD_DIM) for h in heads]

    def mask_top(x, diag):
        if not diag:
            return x
        top = jnp.where(causal, x[:CHUNK], 0.0)
        return top if x.shape[0] == CHUNK else jnp.concatenate([top, x[CHUNK:]], axis=0)

    def front(j, rows, diag):
        start = pl.multiple_of(j * CHUNK, CHUNK)
        zs = [lax.dot_general(q_ref[0, rows, sls[h]], k_ref[0, pl.ds(start, CHUNK), sls[h]], _NT,
                              preferred_element_type=F32) for h in heads]
        log_betas, hls = [], []
        for h in heads:
            z = zs[h]
            e = jnp.log(1.0 + jnp.exp2(-jnp.abs(z))) * LOG2E
            log_beta = jnp.minimum(z, 0.0) - e
            log_1mb = mask_top(log_beta - z, diag)
            log_betas.append(log_beta)
            hls.append(log_1mb.astype(BF16))
        rs = [jnp.dot(hls[h], tri_ref[...], preferred_element_type=F32) for h in heads]
        return log_betas, rs

    def back(j, rows, diag, log_betas, rs):
        start = pl.multiple_of(j * CHUNK, CHUNK)
        for h in heads:
            a = mask_top(jnp.exp2(log_betas[h] + rs[h][:, :CHUNK] + carry_ref[h, rows, :]), diag)
            acc_ref[h, rows, :] += jnp.dot(a.astype(BF16), v_ref[0, pl.ds(start, CHUNK), sls[h]],
                                           preferred_element_type=F32)
            carry_ref[h, rows, :] += rs[h][:, CHUNK:]

    first = blk * SB_Q_TILES
    diag_steps = [(first + (SB_Q_TILES - 1) - d, slice((SB_Q_TILES - 1 - d) * CHUNK, SB_Q_BLOCK))
                  for d in range(SB_Q_TILES)]
    fronts = [front(j, rows, True) for j, rows in diag_steps]
    for (j, rows), halves in zip(diag_steps, fronts):
        back(j, rows, True, *halves)

    def live(rows):
        return (jnp.max(carry_ref[:, rows, :]) > SB_DEAD_LOG2).astype(jnp.int32)

    lower, upper, whole = slice(0, SB_HALF), slice(SB_HALF, SB_Q_BLOCK), slice(0, SB_Q_BLOCK)

    def stage(j, rows):
        log_betas, rs = front(jnp.maximum(j, 0), rows, False)
        for h in heads:
            lb_ref[h, rows, :] = log_betas[h]
            r_ref[h, rows, :] = rs[h]

    def below_step(j, rows):
        rs = [r_ref[h, rows, :] for h in heads]
        after = [carry_ref[h, rows, :] + rs[h][:, CHUNK:] for h in heads]
        peak = functools.reduce(jnp.maximum, after)
        alive = [(jnp.max(peak[part, :]) > SB_DEAD_LOG2).astype(jnp.int32)
                 for part in (lower, upper) if part.stop <= peak.shape[0]]
        back(j, rows, False, [lb_ref[h, rows, :] for h in heads], rs)
        stage(j - 1, rows)
        return alive + [jnp.int32(0)] * (2 - len(alive))

    def on_rows(upper_alive, fn):
        return lax.cond(upper_alive > 0, lambda: fn(whole), lambda: fn(lower))

    def first_stage(rows):
        stage(first - 1, rows)
        return 0

    upper_alive = live(upper)
    on_rows(upper_alive, first_stage)

    def below_cond(c):
        jj, lower_alive, upper_alive = c
        return jnp.logical_and(jj < first, lower_alive + upper_alive > 0)

    def below_body(c):
        jj, _, upper_alive = c
        lower_alive, upper_alive = on_rows(upper_alive, lambda rows: below_step(first - 1 - jj, rows))
        return jj + 1, lower_alive, upper_alive

    lax.while_loop(below_cond, below_body, (jnp.int32(0), live(lower), upper_alive))

    for h in heads:
        y_ref[0, :, sls[h]] = (acc_ref[h] * _silu(z_ref[0, :, sls[h]].astype(F32))).astype(y_ref.dtype)


def _sb(proj, tri):
    bsz, s_len, _ = proj.shape
    off_b = 3 * D_A // D_B

    def row_spec(col):
        return pl.BlockSpec((1, SB_Q_BLOCK, D_B), lambda b, i, c=col: (b, i, c))

    def seq_spec(col):
        return pl.BlockSpec((1, s_len, D_B), lambda b, i, c=col: (b, 0, c))

    return pl.pallas_call(
        _sb_kernel,
        grid=(bsz, s_len // SB_Q_BLOCK),
        in_specs=[
            row_spec(off_b), seq_spec(off_b + 1), seq_spec(off_b + 2), row_spec(off_b + 3),
            pl.BlockSpec((CHUNK, 2 * CHUNK), lambda b, i: (0, 0)),
        ],
        out_specs=pl.BlockSpec((1, SB_Q_BLOCK, D_B), lambda b, i: (b, i, 0)),
        out_shape=jax.ShapeDtypeStruct((bsz, s_len, D_B), BF16),
        scratch_shapes=[pltpu.VMEM((B_HEADS, SB_Q_BLOCK, B_HEAD_DIM), F32),
                        pltpu.VMEM((B_HEADS, SB_Q_BLOCK, B_HEAD_DIM), F32),
                        pltpu.VMEM((B_HEADS, SB_Q_BLOCK, CHUNK), F32),
                        pltpu.VMEM((B_HEADS, SB_Q_BLOCK, 2 * CHUNK), F32)],
        compiler_params=_params(2),
        name="sb",
    )(proj, proj, proj, proj, tri)


def _cumsum_matrix():
    j = lax.broadcasted_iota(jnp.int32, (CHUNK, CHUNK), 0)
    s = lax.broadcasted_iota(jnp.int32, (CHUNK, CHUNK), 1)
    return jnp.concatenate([(j > s).astype(BF16), jnp.ones((CHUNK, CHUNK), BF16)], axis=1)


def kernel(x, mem, norm_g, w_in, sgu_ln_g, sgu_ln_b, sgu_w, sgu_b, mem_norm_g, w_mem_kv,
           q_norm_g, k_norm_g, w_out):
    bsz, s_len, _ = x.shape
    depth = w_in.shape[0]
    assert s_len % SB_Q_BLOCK == 0 and x.shape[2] == D_MODEL and mem.shape[1] == MEM_LEN
    m = bsz * s_len

    tri = _cumsum_matrix()
    col_scale = jnp.ones((1, IN_WIDTH), F32).at[:, 3 * D_A:3 * D_A + D_B].set(
        LOG2E / math.sqrt(B_HEAD_DIM))
    tril = jnp.tril(jnp.ones((CHUNK, CHUNK), dtype=bool))
    memk, memv = _memkv(mem, mem_norm_g, w_mem_kv.astype(BF16), k_norm_g)
    w_in_bf16 = w_in.astype(BF16)
    w_out_bf16 = w_out.astype(BF16)

    x2d = x.reshape(m, D_MODEL)
    for l in range(depth):
        proj = _inproj(x2d, norm_g[l].reshape(1, D_MODEL), w_in_bf16, l, col_scale)
        w_tril = jnp.where(tril[None], sgu_w[l], 0.0).astype(BF16)
        bias_full = jnp.repeat(sgu_b[l].T, A_GROUP_DIM, axis=1)
        yb = _sb(proj.reshape(bsz, s_len, IN_WIDTH), tri)
        x2d = _mixout(proj, yb.reshape(m, D_B), memk, memv, l, s_len,
                      sgu_ln_g[l].reshape(1, D_A), sgu_ln_b[l].reshape(1, D_A),
                      w_tril, bias_full, q_norm_g[l].reshape(1, C_HEAD_DIM), w_out_bf16, x2d)
    return x2d.reshape(bsz, s_len, D_MODEL)
```

```python
import functools
import math

import jax
import jax.numpy as jnp
from jax import lax
from jax.experimental import pallas as pl
from jax.experimental.pallas import tpu as pltpu

D_MODEL = 2048
MEM_LEN = 256
CHUNK = 128
D_A = D_MODEL // 2
A_GROUPS = 8
A_GROUP_DIM = D_A // A_GROUPS
D_B = D_MODEL // 4
B_HEADS = 4
B_HEAD_DIM = D_B // B_HEADS
D_C = D_MODEL // 4
C_HEADS = 4
C_HEAD_DIM = D_C // C_HEADS
IN_WIDTH = 3 * D_A + 4 * D_B + 2 * D_C
EPS = 1e-6
LOG2E = math.log2(math.e)

SB_Q_TILES = 4
SB_Q_BLOCK = SB_Q_TILES * CHUNK
SB_HALF = SB_Q_BLOCK // 2
SB_DEAD_LOG2 = -150.0
IP_ROWS = 512
IP_PIECE = 128
IP_COLS = 2048
MO_ROWS = 512
MO_PIECE = 128

F32 = jnp.float32
BF16 = jnp.bfloat16

VMEM_LIMIT_BYTES = 56 * 1024 * 1024

_NT = (((1,), (1,)), ((), ()))


def _gelu(x):
    return 0.5 * x * (1.0 + lax.erf(x * math.sqrt(0.5)))


def _silu(x):
    return x / (1.0 + jnp.exp(-x))


def _params(n_grid_dims):
    return pltpu.CompilerParams(
        dimension_semantics=("arbitrary",) * n_grid_dims,
        vmem_limit_bytes=VMEM_LIMIT_BYTES)


def _inproj_kernel(x_ref, g_ref, w_ref, cs_ref, o_ref):
    def norm(p):
        x = x_ref[p * IP_PIECE:(p + 1) * IP_PIECE, :]
        ms = jnp.mean(x * x, axis=-1, keepdims=True)
        return (x * lax.rsqrt(ms + EPS) * g_ref[...]).astype(BF16)

    def project(p, h):
        rows = slice(p * IP_PIECE, (p + 1) * IP_PIECE)
        for c in range(IN_WIDTH // IP_COLS):
            cols = slice(c * IP_COLS, (c + 1) * IP_COLS)
            acc = jnp.dot(h, w_ref[:, cols], preferred_element_type=F32)
            o_ref[rows, cols] = (acc * cs_ref[:, cols]).astype(o_ref.dtype)

    pieces = IP_ROWS // IP_PIECE
    ready = norm(0)
    for p in range(pieces):
        nxt = norm(p + 1) if p + 1 < pieces else None
        project(p, ready)
        ready = nxt


def _inproj(x2d, g, w_bf16, layer, col_scale):
    m, k = x2d.shape
    n = w_bf16.shape[2]
    return pl.pallas_call(
        _inproj_kernel,
        grid=(m // IP_ROWS,),
        in_specs=[
            pl.BlockSpec((IP_ROWS, k), lambda i: (i, 0)),
            pl.BlockSpec((1, k), lambda i: (0, 0)),
            pl.BlockSpec((None, k, n), lambda i: (layer, 0, 0), pipeline_mode=pl.Buffered(1)),
            pl.BlockSpec((1, n), lambda i: (0, 0)),
        ],
        out_specs=pl.BlockSpec((IP_ROWS, n), lambda i: (i, 0)),
        out_shape=jax.ShapeDtypeStruct((m, n), BF16),
        compiler_params=_params(1),
        name="inproj",
    )(x2d, g, w_bf16, col_scale)


def _memkv_kernel(mem_ref, g_ref, w_ref, kg_ref, k_ref, v_ref):
    x = mem_ref[0]
    ms = jnp.mean(x * x, axis=-1, keepdims=True)
    h = (x * lax.rsqrt(ms + EPS) * g_ref[0]).astype(BF16)
    kv = jnp.dot(h, w_ref[0], preferred_element_type=F32)
    for hd in range(C_HEADS):
        sl = slice(hd * C_HEAD_DIM, (hd + 1) * C_HEAD_DIM)
        kh = kv[:, sl]
        kms = jnp.mean(kh * kh, axis=-1, keepdims=True)
        k_ref[0, 0, :, sl] = (kh * lax.rsqrt(kms + EPS) * kg_ref[0]).astype(BF16)
    v_ref[0, 0] = kv[:, D_C:].astype(BF16)


def _memkv(mem, mem_norm_g, w_mem_kv_bf16, k_norm_g):
    bsz = mem.shape[0]
    depth = w_mem_kv_bf16.shape[0]
    out = jax.ShapeDtypeStruct((depth, bsz, MEM_LEN, D_C), BF16)
    return pl.pallas_call(
        _memkv_kernel,
        grid=(depth, bsz),
        in_specs=[
            pl.BlockSpec((1, MEM_LEN, D_MODEL), lambda l, b: (b, 0, 0)),
            pl.BlockSpec((1, 1, D_MODEL), lambda l, b: (l, 0, 0)),
            pl.BlockSpec((1, D_MODEL, 2 * D_C), lambda l, b: (l, 0, 0)),
            pl.BlockSpec((1, 1, C_HEAD_DIM), lambda l, b: (l, 0, 0)),
        ],
        out_specs=[
            pl.BlockSpec((1, 1, MEM_LEN, D_C), lambda l, b: (l, b, 0, 0)),
            pl.BlockSpec((1, 1, MEM_LEN, D_C), lambda l, b: (l, b, 0, 0)),
        ],
        out_shape=[out, out],
        compiler_params=_params(2),
        name="memkv",
    )(mem, mem_norm_g.reshape(depth, 1, D_MODEL), w_mem_kv_bf16,
      k_norm_g.reshape(depth, 1, C_HEAD_DIM))


def _mixout_kernel(u_ref, v_ref, za_ref, qc_ref, zc_ref, yb_ref, mk_ref, mv_ref,
                   lng_ref, lnb_ref, w_ref, bias_ref, qg_ref, w_out_ref, x_ref, o_ref):
    def mix(c):
        rows = slice(c * CHUNK, (c + 1) * CHUNK)
        v = _gelu(v_ref[rows, :].astype(F32))
        mu = jnp.mean(v, axis=-1, keepdims=True)
        vc = v - mu
        var = jnp.mean(vc * vc, axis=-1, keepdims=True)
        vn = (vc * lax.rsqrt(var + EPS) * lng_ref[...] + lnb_ref[...]).astype(BF16)
        ya = []
        for g in range(A_GROUPS):
            sl = slice(g * A_GROUP_DIM, (g + 1) * A_GROUP_DIM)
            mixed = jnp.dot(w_ref[g], vn[:, sl], preferred_element_type=F32) + bias_ref[:, sl]
            u = _gelu(u_ref[rows, sl].astype(F32))
            ya.append((u * mixed * _silu(za_ref[rows, sl].astype(F32))).astype(BF16))
        yc = []
        for h in range(C_HEADS):
            sl = slice(h * C_HEAD_DIM, (h + 1) * C_HEAD_DIM)
            q = qc_ref[rows, sl].astype(F32)
            qms = jnp.mean(q * q, axis=-1, keepdims=True)
            qn = (q * lax.rsqrt(qms + EPS) * qg_ref[...]).astype(BF16)
            s = lax.dot_general(qn, mk_ref[:, sl], _NT,
                                preferred_element_type=F32) * (1.0 / math.sqrt(C_HEAD_DIM))
            p = jnp.exp(s - jnp.max(s, axis=-1, keepdims=True))
            denom = jnp.sum(p, axis=-1, keepdims=True)
            o = jnp.dot(p.astype(BF16), mv_ref[:, sl], preferred_element_type=F32) / denom
            yc.append((o * _silu(zc_ref[rows, sl].astype(F32))).astype(BF16))
        return jnp.concatenate(ya + [yb_ref[rows, :]] + yc, axis=1)

    def project(p, ys):
        rows = slice(p * MO_PIECE, (p + 1) * MO_PIECE)
        y = jnp.concatenate(ys, axis=0)
        o_ref[rows, :] = x_ref[rows, :] + jnp.dot(y, w_out_ref[...], preferred_element_type=F32)

    per_piece = MO_PIECE // CHUNK
    pieces = MO_ROWS // MO_PIECE
    mix_piece = lambda p: [mix(p * per_piece + c) for c in range(per_piece)]
    ready = mix_piece(0)
    for p in range(pieces):
        nxt = mix_piece(p + 1) if p + 1 < pieces else None
        project(p, ready)
        ready = nxt


def _mixout(proj2d, yb2d, memk, memv, layer, s_len, lng, lnb, w_tril, bias_full, qg,
            w_out_bf16, x2d):
    m = x2d.shape[0]
    off_b = 3 * D_A // D_B
    blocks_per_seq = s_len // MO_ROWS

    def row_spec(width, col):
        return pl.BlockSpec((MO_ROWS, width), lambda i, c=col: (i, c))

    def const_spec(shape):
        nd = len(shape)
        return pl.BlockSpec(shape, lambda i, nd=nd: (0,) * nd)

    mem_spec = pl.BlockSpec((None, None, MEM_LEN, D_C),
                            lambda i: (layer, i // blocks_per_seq, 0, 0))
    return pl.pallas_call(
        _mixout_kernel,
        grid=(m // MO_ROWS,),
        in_specs=[
            row_spec(D_A, 0), row_spec(D_A, 1), row_spec(D_A, 2),
            row_spec(D_C, off_b + 4), row_spec(D_C, off_b + 5),
            row_spec(D_B, 0),
            mem_spec, mem_spec,
            const_spec((1, D_A)), const_spec((1, D_A)),
            const_spec((A_GROUPS, CHUNK, CHUNK)), const_spec((CHUNK, D_A)),
            const_spec((1, C_HEAD_DIM)),
            pl.BlockSpec((None, D_MODEL, D_MODEL), lambda i: (layer, 0, 0)),
            row_spec(D_MODEL, 0),
        ],
        out_specs=pl.BlockSpec((MO_ROWS, D_MODEL), lambda i: (i, 0)),
        out_shape=jax.ShapeDtypeStruct((m, D_MODEL), F32),
        compiler_params=_params(1),
        name="mixout",
    )(proj2d, proj2d, proj2d, proj2d, proj2d, yb2d, memk, memv, lng, lnb, w_tril, bias_full, qg,
      w_out_bf16, x2d)


def _sb_kernel(q_ref, k_ref, v_ref, z_ref, tri_ref, y_ref, carry_ref, acc_ref, lb_ref, r_ref):
    blk = pl.program_id(1)
    carry_ref[...] = jnp.zeros_like(carry_ref)
    acc_ref[...] = jnp.zeros_like(acc_ref)
    row = lax.broadcasted_iota(jnp.int32, (CHUNK, CHUNK), 0)
    col = lax.broadcasted_iota(jnp.int32, (CHUNK, CHUNK), 1)
    causal = col < row
    heads = range(B_HEADS)
    sls = [slice(h * B_HEAD_DIM, (h + 1) * B_HEAD_DIM) for h in heads]

    def mask_top(x, diag):
        if not diag:
            return x
        top = jnp.where(causal, x[:CHUNK], 0.0)
        return top if x.shape[0] == CHUNK else jnp.concatenate([top, x[CHUNK:]], axis=0)

    def front(j, rows, diag):
        start = pl.multiple_of(j * CHUNK, CHUNK)
        zs = [lax.dot_general(q_ref[0, rows, sls[h]], k_ref[0, pl.ds(start, CHUNK), sls[h]], _NT,
                              preferred_element_type=F32) for h in heads]
        log_betas, hls = [], []
        for h in heads:
            z = zs[h]
            e = jnp.log(1.0 + jnp.exp2(-jnp.abs(z))) * LOG2E
            log_beta = jnp.minimum(z, 0.0) - e
            log_1mb = mask_top(log_beta - z, diag)
            log_betas.append(log_beta)
            hls.append(log_1mb.astype(BF16))
        rs = [jnp.dot(hls[h], tri_ref[...], preferred_element_type=F32) for h in heads]
        return log_betas, rs

    def back(j, rows, diag, log_betas, rs):
        start = pl.multiple_of(j * CHUNK, CHUNK)
        for h in heads:
            a = mask_top(jnp.exp2(log_betas[h] + rs[h][:, :CHUNK] + carry_ref[h, rows, :]), diag)
            acc_ref[h, rows, :] += jnp.dot(a.astype(BF16), v_ref[0, pl.ds(start, CHUNK), sls[h]],
                                           preferred_element_type=F32)
            carry_ref[h, rows, :] += rs[h][:, CHUNK:]

    first = blk * SB_Q_TILES
    diag_steps = [(first + (SB_Q_TILES - 1) - d, slice((SB_Q_TILES - 1 - d) * CHUNK, SB_Q_BLOCK))
                  for d in range(SB_Q_TILES)]
    fronts = [front(j, rows, True) for j, rows in diag_steps]
    for (j, rows), halves in zip(diag_steps, fronts):
        back(j, rows, True, *halves)

    def live(rows):
        return (jnp.max(carry_ref[:, rows, :]) > SB_DEAD_LOG2).astype(jnp.int32)

    lower, upper, whole = slice(0, SB_HALF), slice(SB_HALF, SB_Q_BLOCK), slice(0, SB_Q_BLOCK)

    def stage(j, rows):
        log_betas, rs = front(jnp.maximum(j, 0), rows, False)
        for h in heads:
            lb_ref[h, rows, :] = log_betas[h]
            r_ref[h, rows, :] = rs[h]

    def below_step(j, rows):
        rs = [r_ref[h, rows, :] for h in heads]
        after = [carry_ref[h, rows, :] + rs[h][:, CHUNK:] for h in heads]
        peak = functools.reduce(jnp.maximum, after)
        alive = [(jnp.max(peak[part, :]) > SB_DEAD_LOG2).astype(jnp.int32)
                 for part in (lower, upper) if part.stop <= peak.shape[0]]
        back(j, rows, False, [lb_ref[h, rows, :] for h in heads], rs)
        stage(j - 1, rows)
        return alive + [jnp.int32(0)] * (2 - len(alive))

    def on_rows(upper_alive, fn):
        return lax.cond(upper_alive > 0, lambda: fn(whole), lambda: fn(lower))

    def first_stage(rows):
        stage(first - 1, rows)
        return 0

    upper_alive = live(upper)
    on_rows(upper_alive, first_stage)

    def below_cond(c):
        jj, lower_alive, upper_alive = c
        return jnp.logical_and(jj < first, lower_alive + upper_alive > 0)

    def below_body(c):
        jj, _, upper_alive = c
        lower_alive, upper_alive = on_rows(upper_alive, lambda rows: below_step(first - 1 - jj, rows))
        return jj + 1, lower_alive, upper_alive

    lax.while_loop(below_cond, below_body, (jnp.int32(0), live(lower), upper_alive))

    for h in heads:
        y_ref[0, :, sls[h]] = (acc_ref[h] * _silu(z_ref[0, :, sls[h]].astype(F32))).astype(y_ref.dtype)


def _sb(proj, tri):
    bsz, s_len, _ = proj.shape
    off_b = 3 * D_A // D_B

    def row_spec(col):
        return pl.BlockSpec((1, SB_Q_BLOCK, D_B), lambda b, i, c=col: (b, i, c))

    def seq_spec(col):
        return pl.BlockSpec((1, s_len, D_B), lambda b, i, c=col: (b, 0, c))

    return pl.pallas_call(
        _sb_kernel,
        grid=(bsz, s_len // SB_Q_BLOCK),
        in_specs=[
            row_spec(off_b), seq_spec(off_b + 1), seq_spec(off_b + 2), row_spec(off_b + 3),
            pl.BlockSpec((CHUNK, 2 * CHUNK), lambda b, i: (0, 0)),
        ],
        out_specs=pl.BlockSpec((1, SB_Q_BLOCK, D_B), lambda b, i: (b, i, 0)),
        out_shape=jax.ShapeDtypeStruct((bsz, s_len, D_B), BF16),
        scratch_shapes=[pltpu.VMEM((B_HEADS, SB_Q_BLOCK, B_HEAD_DIM), F32),
                        pltpu.VMEM((B_HEADS, SB_Q_BLOCK, B_HEAD_DIM), F32),
                        pltpu.VMEM((B_HEADS, SB_Q_BLOCK, CHUNK), F32),
                        pltpu.VMEM((B_HEADS, SB_Q_BLOCK, 2 * CHUNK), F32)],
        compiler_params=_params(2),
        name="sb",
    )(proj, proj, proj, proj, tri)


def _cumsum_matrix():
    j = lax.broadcasted_iota(jnp.int32, (CHUNK, CHUNK), 0)
    s = lax.broadcasted_iota(jnp.int32, (CHUNK, CHUNK), 1)
    return jnp.concatenate([(j > s).astype(BF16), jnp.ones((CHUNK, CHUNK), BF16)], axis=1)


def kernel(x, mem, norm_g, w_in, sgu_ln_g, sgu_ln_b, sgu_w, sgu_b, mem_norm_g, w_mem_kv,
           q_norm_g, k_norm_g, w_out):
    bsz, s_len, _ = x.shape
    depth = w_in.shape[0]
    assert s_len % SB_Q_BLOCK == 0 and x.shape[2] == D_MODEL and mem.shape[1] == MEM_LEN
    m = bsz * s_len

    tri = _cumsum_matrix()
    col_scale = jnp.ones((1, IN_WIDTH), F32).at[:, 3 * D_A:3 * D_A + D_B].set(
        LOG2E / math.sqrt(B_HEAD_DIM))
    tril = jnp.tril(jnp.ones((CHUNK, CHUNK), dtype=bool))
    memk, memv = _memkv(mem, mem_norm_g, w_mem_kv.astype(BF16), k_norm_g)
    w_in_bf16 = w_in.astype(BF16)
    w_out_bf16 = w_out.astype(BF16)

    x2d = x.reshape(m, D_MODEL)
    for l in range(depth):
        proj = _inproj(x2d, norm_g[l].reshape(1, D_MODEL), w_in_bf16, l, col_scale)
        w_tril = jnp.where(tril[None], sgu_w[l], 0.0).astype(BF16)
        bias_full = jnp.repeat(sgu_b[l].T, A_GROUP_DIM, axis=1)
        yb = _sb(proj.reshape(bsz, s_len, IN_WIDTH), tri)
        x2d = _mixout(proj, yb.reshape(m, D_B), memk, memv, l, s_len,
                      sgu_ln_g[l].reshape(1, D_A), sgu_ln_b[l].reshape(1, D_A),
                      w_tril, bias_full, q_norm_g[l].reshape(1, C_HEAD_DIM), w_out_bf16, x2d)
    return x2d.reshape(bsz, s_len, D_MODEL)
```

```python
import functools
import math

import jax
import jax.numpy as jnp
from jax import lax
from jax.experimental import pallas as pl
from jax.experimental.pallas import tpu as pltpu

D_MODEL = 2048
MEM_LEN = 256
CHUNK = 128
D_A = D_MODEL // 2
A_GROUPS = 8
A_GROUP_DIM = D_A // A_GROUPS
D_B = D_MODEL // 4
B_HEADS = 4
B_HEAD_DIM = D_B // B_HEADS
D_C = D_MODEL // 4
C_HEADS = 4
C_HEAD_DIM = D_C // C_HEADS
IN_WIDTH = 3 * D_A + 4 * D_B + 2 * D_C
IN_SEGMENTS = (
    (0, D_A, "gelu"), (D_A, D_A, "gelu_ln"), (2 * D_A, D_A, "silu"),
    (3 * D_A, D_B, "sb_q"), (3 * D_A + D_B, D_B, "id"), (3 * D_A + 2 * D_B, D_B, "id"),
    (3 * D_A + 3 * D_B, D_B, "silu"),
    (3 * D_A + 4 * D_B, D_C, "mem_q"), (3 * D_A + 4 * D_B + D_C, D_C, "silu"),
)
EPS = 1e-6
LOG2E = math.log2(math.e)

SB_Q_TILES = 4
SB_Q_BLOCK = SB_Q_TILES * CHUNK
SB_HALF = SB_Q_BLOCK // 2
SB_DEAD_LOG2 = -150.0
IP_ROWS = 512
IP_PIECE = 256
IP_COLS = 2048
MO_ROWS = 512
MO_PIECE = 256

F32 = jnp.float32
BF16 = jnp.bfloat16

VMEM_LIMIT_BYTES = 56 * 1024 * 1024

_NT = (((1,), (1,)), ((), ()))


def _gelu(x):
    return 0.5 * x * (1.0 + lax.erf(x * math.sqrt(0.5)))


def _silu(x):
    return x / (1.0 + jnp.exp(-x))


def _params(n_grid_dims):
    return pltpu.CompilerParams(
        dimension_semantics=("arbitrary",) * n_grid_dims,
        vmem_limit_bytes=VMEM_LIMIT_BYTES)


def _inproj_kernel(x_ref, g_ref, w_ref, lng_ref, lnb_ref, qg_ref, o_ref):
    def norm(p):
        x = x_ref[p * IP_PIECE:(p + 1) * IP_PIECE, :]
        ms = jnp.mean(x * x, axis=-1, keepdims=True)
        return (x * lax.rsqrt(ms + EPS) * g_ref[...]).astype(BF16)

    def activate(kind, t):
        if kind == "gelu":
            return _gelu(t)
        if kind == "gelu_ln":
            v = _gelu(t)
            vc = v - jnp.mean(v, axis=-1, keepdims=True)
            var = jnp.mean(vc * vc, axis=-1, keepdims=True)
            return vc * lax.rsqrt(var + EPS) * lng_ref[...] + lnb_ref[...]
        if kind == "silu":
            return _silu(t)
        if kind == "sb_q":
            return t * (LOG2E / math.sqrt(B_HEAD_DIM))
        if kind == "mem_q":
            heads = []
            for h in range(C_HEADS):
                q = t[:, h * C_HEAD_DIM:(h + 1) * C_HEAD_DIM]
                qms = jnp.mean(q * q, axis=-1, keepdims=True)
                heads.append(q * lax.rsqrt(qms + EPS) * (qg_ref[...] / math.sqrt(C_HEAD_DIM)))
            return jnp.concatenate(heads, axis=1)
        assert kind == "id"
        return t

    def project(p, h):
        rows = slice(p * IP_PIECE, (p + 1) * IP_PIECE)
        for c in range(IN_WIDTH // IP_COLS):
            acc = jnp.dot(h, w_ref[:, c * IP_COLS:(c + 1) * IP_COLS], preferred_element_type=F32)
            for start, width, kind in IN_SEGMENTS:
                if c * IP_COLS <= start < (c + 1) * IP_COLS:
                    t = acc[:, start - c * IP_COLS:start - c * IP_COLS + width]
                    o_ref[rows, start:start + width] = activate(kind, t).astype(o_ref.dtype)

    pieces = IP_ROWS // IP_PIECE
    ready = norm(0)
    for p in range(pieces):
        nxt = norm(p + 1) if p + 1 < pieces else None
        project(p, ready)
        ready = nxt


def _inproj(x2d, g, w_bf16, layer, lng, lnb, qg):
    m, k = x2d.shape
    n = w_bf16.shape[2]
    assert n == IN_WIDTH and all(w % CHUNK == 0 and s // IP_COLS == (s + w - 1) // IP_COLS
                                 for s, w, _ in IN_SEGMENTS)
    return pl.pallas_call(
        _inproj_kernel,
        grid=(m // IP_ROWS,),
        in_specs=[
            pl.BlockSpec((IP_ROWS, k), lambda i: (i, 0)),
            pl.BlockSpec((1, k), lambda i: (0, 0)),
            pl.BlockSpec((None, k, n), lambda i: (layer, 0, 0), pipeline_mode=pl.Buffered(1)),
            pl.BlockSpec((1, D_A), lambda i: (0, 0)),
            pl.BlockSpec((1, D_A), lambda i: (0, 0)),
            pl.BlockSpec((1, C_HEAD_DIM), lambda i: (0, 0)),
        ],
        out_specs=pl.BlockSpec((IP_ROWS, n), lambda i: (i, 0)),
        out_shape=jax.ShapeDtypeStruct((m, n), BF16),
        compiler_params=_params(1),
        name="inproj",
    )(x2d, g, w_bf16, lng, lnb, qg)


def _memkv_kernel(mem_ref, g_ref, w_ref, kg_ref, k_ref, v_ref):
    x = mem_ref[0]
    ms = jnp.mean(x * x, axis=-1, keepdims=True)
    h = (x * lax.rsqrt(ms + EPS) * g_ref[0]).astype(BF16)
    kv = jnp.dot(h, w_ref[0], preferred_element_type=F32)
    for hd in range(C_HEADS):
        sl = slice(hd * C_HEAD_DIM, (hd + 1) * C_HEAD_DIM)
        kh = kv[:, sl]
        kms = jnp.mean(kh * kh, axis=-1, keepdims=True)
        k_ref[0, 0, :, sl] = (kh * lax.rsqrt(kms + EPS) * kg_ref[0]).astype(BF16)
    v_ref[0, 0] = kv[:, D_C:].astype(BF16)


def _memkv(mem, mem_norm_g, w_mem_kv_bf16, k_norm_g):
    bsz = mem.shape[0]
    depth = w_mem_kv_bf16.shape[0]
    out = jax.ShapeDtypeStruct((depth, bsz, MEM_LEN, D_C), BF16)
    return pl.pallas_call(
        _memkv_kernel,
        grid=(depth, bsz),
        in_specs=[
            pl.BlockSpec((1, MEM_LEN, D_MODEL), lambda l, b: (b, 0, 0)),
            pl.BlockSpec((1, 1, D_MODEL), lambda l, b: (l, 0, 0)),
            pl.BlockSpec((1, D_MODEL, 2 * D_C), lambda l, b: (l, 0, 0)),
            pl.BlockSpec((1, 1, C_HEAD_DIM), lambda l, b: (l, 0, 0)),
        ],
        out_specs=[
            pl.BlockSpec((1, 1, MEM_LEN, D_C), lambda l, b: (l, b, 0, 0)),
            pl.BlockSpec((1, 1, MEM_LEN, D_C), lambda l, b: (l, b, 0, 0)),
        ],
        out_shape=[out, out],
        compiler_params=_params(2),
        name="memkv",
    )(mem, mem_norm_g.reshape(depth, 1, D_MODEL), w_mem_kv_bf16,
      k_norm_g.reshape(depth, 1, C_HEAD_DIM))


def _mixout_kernel(u_ref, v_ref, za_ref, qc_ref, zc_ref, yb_ref, mk_ref, mv_ref,
                   w_ref, bias_ref, w_out_ref, x_ref, o_ref):
    def mix(c):
        rows = slice(c * CHUNK, (c + 1) * CHUNK)
        ya = []
        for g in range(A_GROUPS):
            sl = slice(g * A_GROUP_DIM, (g + 1) * A_GROUP_DIM)
            mixed = jnp.dot(w_ref[g], v_ref[rows, sl], preferred_element_type=F32) + bias_ref[:, sl]
            gate = u_ref[rows, sl].astype(F32) * za_ref[rows, sl].astype(F32)
            ya.append((gate * mixed).astype(BF16))
        yc = []
        for h in range(C_HEADS):
            sl = slice(h * C_HEAD_DIM, (h + 1) * C_HEAD_DIM)
            s = lax.dot_general(qc_ref[rows, sl], mk_ref[:, sl], _NT, preferred_element_type=F32)
            p = jnp.exp(s - jnp.max(s, axis=-1, keepdims=True))
            denom = jnp.sum(p, axis=-1, keepdims=True)
            o = jnp.dot(p.astype(BF16), mv_ref[:, sl], preferred_element_type=F32) / denom
            yc.append((o * zc_ref[rows, sl].astype(F32)).astype(BF16))
        return jnp.concatenate(ya + [yb_ref[rows, :]] + yc, axis=1)

    def project(p, ys):
        rows = slice(p * MO_PIECE, (p + 1) * MO_PIECE)
        y = jnp.concatenate(ys, axis=0)
        o_ref[rows, :] = x_ref[rows, :] + jnp.dot(y, w_out_ref[...], preferred_element_type=F32)

    per_piece = MO_PIECE // CHUNK
    pieces = MO_ROWS // MO_PIECE
    mix_piece = lambda p: [mix(p * per_piece + c) for c in range(per_piece)]
    ready = mix_piece(0)
    for p in range(pieces):
        nxt = mix_piece(p + 1) if p + 1 < pieces else None
        project(p, ready)
        ready = nxt


def _mixout(proj2d, yb2d, memk, memv, layer, s_len, w_tril, bias_full, w_out_bf16, x2d):
    m = x2d.shape[0]
    off_b = 3 * D_A // D_B
    blocks_per_seq = s_len // MO_ROWS

    def row_spec(width, col):
        return pl.BlockSpec((MO_ROWS, width), lambda i, c=col: (i, c))

    def const_spec(shape):
        nd = len(shape)
        return pl.BlockSpec(shape, lambda i, nd=nd: (0,) * nd)

    mem_spec = pl.BlockSpec((None, None, MEM_LEN, D_C),
                            lambda i: (layer, i // blocks_per_seq, 0, 0))
    return pl.pallas_call(
        _mixout_kernel,
        grid=(m // MO_ROWS,),
        in_specs=[
            row_spec(D_A, 0), row_spec(D_A, 1), row_spec(D_A, 2),
            row_spec(D_C, off_b + 4), row_spec(D_C, off_b + 5),
            row_spec(D_B, 0),
            mem_spec, mem_spec,
            const_spec((A_GROUPS, CHUNK, CHUNK)), const_spec((CHUNK, D_A)),
            pl.BlockSpec((None, D_MODEL, D_MODEL), lambda i: (layer, 0, 0)),
            row_spec(D_MODEL, 0),
        ],
        out_specs=pl.BlockSpec((MO_ROWS, D_MODEL), lambda i: (i, 0)),
        out_shape=jax.ShapeDtypeStruct((m, D_MODEL), F32),
        compiler_params=_params(1),
        name="mixout",
    )(proj2d, proj2d, proj2d, proj2d, proj2d, yb2d, memk, memv, w_tril, bias_full,
      w_out_bf16, x2d)


def _sb_kernel(q_ref, k_ref, v_ref, z_ref, tri_ref, y_ref, carry_ref, acc_ref, lb_ref, r_ref):
    blk = pl.program_id(1)
    carry_ref[...] = jnp.zeros_like(carry_ref)
    acc_ref[...] = jnp.zeros_like(acc_ref)
    row = lax.broadcasted_iota(jnp.int32, (CHUNK, CHUNK), 0)
    col = lax.broadcasted_iota(jnp.int32, (CHUNK, CHUNK), 1)
    causal = col < row
    heads = range(B_HEADS)
    sls = [slice(h * B_HEAD_DIM, (h + 1) * B_HEAD_DIM) for h in heads]

    def mask_top(x, diag):
        if not diag:
            return x
        top = jnp.where(causal, x[:CHUNK], 0.0)
        return top if x.shape[0] == CHUNK else jnp.concatenate([top, x[CHUNK:]], axis=0)

    def front(j, rows, diag):
        start = pl.multiple_of(j * CHUNK, CHUNK)
        zs = [lax.dot_general(q_ref[0, rows, sls[h]], k_ref[0, pl.ds(start, CHUNK), sls[h]], _NT,
                              preferred_element_type=F32) for h in heads]
        log_betas, hls = [], []
        for h in heads:
            z = zs[h]
            e = jnp.log(1.0 + jnp.exp2(-jnp.abs(z))) * LOG2E
            log_beta = jnp.minimum(z, 0.0) - e
            log_1mb = mask_top(log_beta - z, diag)
            log_betas.append(log_beta)
            hls.append(log_1mb.astype(BF16))
        rs = [jnp.dot(hls[h], tri_ref[...], preferred_element_type=F32) for h in heads]
        return log_betas, rs

    def back(j, rows, diag, log_betas, rs):
        start = pl.multiple_of(j * CHUNK, CHUNK)
        for h in heads:
            a = mask_top(jnp.exp2(log_betas[h] + rs[h][:, :CHUNK] + carry_ref[h, rows, :]), diag)
            acc_ref[h, rows, :] += jnp.dot(a.astype(BF16), v_ref[0, pl.ds(start, CHUNK), sls[h]],
                                           preferred_element_type=F32)
            carry_ref[h, rows, :] += rs[h][:, CHUNK:]

    first = blk * SB_Q_TILES
    diag_steps = [(first + (SB_Q_TILES - 1) - d, slice((SB_Q_TILES - 1 - d) * CHUNK, SB_Q_BLOCK))
                  for d in range(SB_Q_TILES)]
    fronts = [front(j, rows, True) for j, rows in diag_steps]
    for (j, rows), halves in zip(diag_steps, fronts):
        back(j, rows, True, *halves)

    def live(rows):
        return (jnp.max(carry_ref[:, rows, :]) > SB_DEAD_LOG2).astype(jnp.int32)

    lower, upper, whole = slice(0, SB_HALF), slice(SB_HALF, SB_Q_BLOCK), slice(0, SB_Q_BLOCK)

    def stage(j, rows):
        log_betas, rs = front(jnp.maximum(j, 0), rows, False)
        for h in heads:
            lb_ref[h, rows, :] = log_betas[h]
            r_ref[h, rows, :] = rs[h]

    def below_step(j, rows):
        rs = [r_ref[h, rows, :] for h in heads]
        after = [carry_ref[h, rows, :] + rs[h][:, CHUNK:] for h in heads]
        peak = functools.reduce(jnp.maximum, after)
        alive = [(jnp.max(peak[part, :]) > SB_DEAD_LOG2).astype(jnp.int32)
                 for part in (lower, upper) if part.stop <= peak.shape[0]]
        back(j, rows, False, [lb_ref[h, rows, :] for h in heads], rs)
        stage(j - 1, rows)
        return alive + [jnp.int32(0)] * (2 - len(alive))

    def on_rows(upper_alive, fn):
        return lax.cond(upper_alive > 0, lambda: fn(whole), lambda: fn(lower))

    def first_stage(rows):
        stage(first - 1, rows)
        return 0

    upper_alive = live(upper)
    on_rows(upper_alive, first_stage)

    def below_cond(c):
        jj, lower_alive, upper_alive = c
        return jnp.logical_and(jj < first, lower_alive + upper_alive > 0)

    def below_body(c):
        jj, _, upper_alive = c
        lower_alive, upper_alive = on_rows(upper_alive, lambda rows: below_step(first - 1 - jj, rows))
        return jj + 1, lower_alive, upper_alive

    lax.while_loop(below_cond, below_body, (jnp.int32(0), live(lower), upper_alive))

    for h in heads:
        y_ref[0, :, sls[h]] = (acc_ref[h] * z_ref[0, :, sls[h]].astype(F32)).astype(y_ref.dtype)


def _sb(proj, tri):
    bsz, s_len, _ = proj.shape
    off_b = 3 * D_A // D_B

    def row_spec(col):
        return pl.BlockSpec((1, SB_Q_BLOCK, D_B), lambda b, i, c=col: (b, i, c))

    def seq_spec(col):
        return pl.BlockSpec((1, s_len, D_B), lambda b, i, c=col: (b, 0, c))

    return pl.pallas_call(
        _sb_kernel,
        grid=(bsz, s_len // SB_Q_BLOCK),
        in_specs=[
            row_spec(off_b), seq_spec(off_b + 1), seq_spec(off_b + 2), row_spec(off_b + 3),
            pl.BlockSpec((CHUNK, 2 * CHUNK), lambda b, i: (0, 0)),
        ],
        out_specs=pl.BlockSpec((1, SB_Q_BLOCK, D_B), lambda b, i: (b, i, 0)),
        out_shape=jax.ShapeDtypeStruct((bsz, s_len, D_B), BF16),
        scratch_shapes=[pltpu.VMEM((B_HEADS, SB_Q_BLOCK, B_HEAD_DIM), F32),
                        pltpu.VMEM((B_HEADS, SB_Q_BLOCK, B_HEAD_DIM), F32),
                        pltpu.VMEM((B_HEADS, SB_Q_BLOCK, CHUNK), F32),
                        pltpu.VMEM((B_HEADS, SB_Q_BLOCK, 2 * CHUNK), F32)],
        compiler_params=_params(2),
        name="sb",
    )(proj, proj, proj, proj, tri)


def _cumsum_matrix():
    j = lax.broadcasted_iota(jnp.int32, (CHUNK, CHUNK), 0)
    s = lax.broadcasted_iota(jnp.int32, (CHUNK, CHUNK), 1)
    return jnp.concatenate([(j > s).astype(BF16), jnp.ones((CHUNK, CHUNK), BF16)], axis=1)


def kernel(x, mem, norm_g, w_in, sgu_ln_g, sgu_ln_b, sgu_w, sgu_b, mem_norm_g, w_mem_kv,
           q_norm_g, k_norm_g, w_out):
    bsz, s_len, _ = x.shape
    depth = w_in.shape[0]
    assert s_len % SB_Q_BLOCK == 0 and x.shape[2] == D_MODEL and mem.shape[1] == MEM_LEN
    m = bsz * s_len

    tri = _cumsum_matrix()
    tril = jnp.tril(jnp.ones((CHUNK, CHUNK), dtype=bool))
    memk, memv = _memkv(mem, mem_norm_g, w_mem_kv.astype(BF16), k_norm_g)
    w_in_bf16 = w_in.astype(BF16)
    w_out_bf16 = w_out.astype(BF16)

    x2d = x.reshape(m, D_MODEL)
    for l in range(depth):
        proj = _inproj(x2d, norm_g[l].reshape(1, D_MODEL), w_in_bf16, l,
                       sgu_ln_g[l].reshape(1, D_A), sgu_ln_b[l].reshape(1, D_A),
                       q_norm_g[l].reshape(1, C_HEAD_DIM))
        w_tril = jnp.where(tril[None], sgu_w[l], 0.0).astype(BF16)
        bias_full = jnp.repeat(sgu_b[l].T, A_GROUP_DIM, axis=1)
        yb = _sb(proj.reshape(bsz, s_len, IN_WIDTH), tri)
        x2d = _mixout(proj, yb.reshape(m, D_B), memk, memv, l, s_len,
                      w_tril, bias_full, w_out_bf16, x2d)
    return x2d.reshape(bsz, s_len, D_MODEL)
```

```python
import functools
import itertools
import math

import jax
import jax.numpy as jnp
from jax import lax
from jax.experimental import pallas as pl
from jax.experimental.pallas import tpu as pltpu

D_MODEL = 2048
MEM_LEN = 256
CHUNK = 128
D_A = D_MODEL // 2
A_GROUPS = 8
A_GROUP_DIM = D_A // A_GROUPS
D_B = D_MODEL // 4
B_HEADS = 4
B_HEAD_DIM = D_B // B_HEADS
D_C = D_MODEL // 4
C_HEADS = 4
C_HEAD_DIM = D_C // C_HEADS
IN_WIDTH = 3 * D_A + 4 * D_B + 2 * D_C
IN_SEGMENTS = (
    (0, D_A, "gelu"), (D_A, D_A, "gelu_ln"), (2 * D_A, D_A, "silu"),
    (3 * D_A, D_B, "sb_q"), (3 * D_A + D_B, D_B, "id"), (3 * D_A + 2 * D_B, D_B, "id"),
    (3 * D_A + 3 * D_B, D_B, "silu"),
    (3 * D_A + 4 * D_B, D_C, "mem_q"), (3 * D_A + 4 * D_B + D_C, D_C, "silu"),
)
EPS = 1e-6
LOG2E = math.log2(math.e)

SB_Q_TILES = 4
SB_Q_BLOCK = SB_Q_TILES * CHUNK
SB_HALF = SB_Q_BLOCK // 2
SB_DEAD_LOG2 = -150.0
IP_PIECES = (256, 256)
IP_ROWS = sum(IP_PIECES)
IP_COLS = 2048
MO_PIECES = (256, 256)
MO_ROWS = sum(MO_PIECES)
MK_ROWS = 1024
BF16_SUBLANES = 16

F32 = jnp.float32
BF16 = jnp.bfloat16

VMEM_LIMIT_BYTES = 56 * 1024 * 1024

_NT = (((1,), (1,)), ((), ()))


def _gelu(x):
    return 0.5 * x * (1.0 + lax.erf(x * math.sqrt(0.5)))


def _silu(x):
    return x / (1.0 + jnp.exp(-x))


def _piece_slices(sizes):
    assert all(size % CHUNK == 0 for size in sizes)
    stops = list(itertools.accumulate(sizes))
    return [slice(stop - size, stop) for size, stop in zip(sizes, stops)]


def _params(n_grid_dims):
    return pltpu.CompilerParams(
        dimension_semantics=("arbitrary",) * n_grid_dims,
        vmem_limit_bytes=VMEM_LIMIT_BYTES)


def _inproj_kernel(cast_next, x_ref, g_ref, w_ref, lng_ref, lnb_ref, qg_ref, *rest):
    if cast_next:
        w_in_f32_ref, w_out_f32_ref, o_ref, w_in_bf16_ref, w_out_bf16_ref = rest
        w_in_bf16_ref[...] = w_in_f32_ref[...].astype(BF16)
        w_out_bf16_ref[...] = w_out_f32_ref[...].astype(BF16)
    else:
        (o_ref,) = rest
    piece_rows = _piece_slices(IP_PIECES)

    def norm(p):
        x = x_ref[piece_rows[p], :]
        ms = jnp.mean(x * x, axis=-1, keepdims=True)
        return (x * lax.rsqrt(ms + EPS) * g_ref[...]).astype(BF16)

    def activate(kind, t):
        if kind == "gelu":
            return _gelu(t)
        if kind == "gelu_ln":
            v = _gelu(t)
            vc = v - jnp.mean(v, axis=-1, keepdims=True)
            var = jnp.mean(vc * vc, axis=-1, keepdims=True)
            return vc * lax.rsqrt(var + EPS) * lng_ref[...] + lnb_ref[...]
        if kind == "silu":
            return _silu(t)
        if kind == "sb_q":
            return t * (LOG2E / math.sqrt(B_HEAD_DIM))
        if kind == "mem_q":
            heads = []
            for h in range(C_HEADS):
                q = t[:, h * C_HEAD_DIM:(h + 1) * C_HEAD_DIM]
                qms = jnp.mean(q * q, axis=-1, keepdims=True)
                heads.append(q * lax.rsqrt(qms + EPS) * (qg_ref[...] / math.sqrt(C_HEAD_DIM)))
            return jnp.concatenate(heads, axis=1)
        assert kind == "id"
        return t

    def project(p, h):
        rows = piece_rows[p]
        for c in range(IN_WIDTH // IP_COLS):
            acc = jnp.dot(h, w_ref[:, c * IP_COLS:(c + 1) * IP_COLS], preferred_element_type=F32)
            for start, width, kind in IN_SEGMENTS:
                if c * IP_COLS <= start < (c + 1) * IP_COLS:
                    t = acc[:, start - c * IP_COLS:start - c * IP_COLS + width]
                    o_ref[rows, start:start + width] = activate(kind, t).astype(o_ref.dtype)

    pieces = len(IP_PIECES)
    ready = norm(0)
    for p in range(pieces):
        nxt = norm(p + 1) if p + 1 < pieces else None
        project(p, ready)
        ready = nxt


def _inproj(x2d, g, w_bf16, lng, lnb, qg, next_weights=None):
    m, k = x2d.shape
    n = w_bf16.shape[1]
    steps = m // IP_ROWS
    assert n == IN_WIDTH and all(w % CHUNK == 0 and s // IP_COLS == (s + w - 1) // IP_COLS
                                 for s, w, _ in IN_SEGMENTS)
    in_specs = [
        pl.BlockSpec((IP_ROWS, k), lambda i: (i, 0)),
        pl.BlockSpec((1, k), lambda i: (0, 0)),
        pl.BlockSpec((k, n), lambda i: (0, 0), pipeline_mode=pl.Buffered(1)),
        pl.BlockSpec((1, D_A), lambda i: (0, 0)),
        pl.BlockSpec((1, D_A), lambda i: (0, 0)),
        pl.BlockSpec((1, C_HEAD_DIM), lambda i: (0, 0)),
    ]
    out_specs = [pl.BlockSpec((IP_ROWS, n), lambda i: (i, 0))]
    out_shape = [jax.ShapeDtypeStruct((m, n), BF16)]
    operands = [x2d, g, w_bf16, lng, lnb, qg]
    if next_weights is not None:
        w_in_f32, w_out_f32, nxt = next_weights
        for w in (w_in_f32, w_out_f32):
            slab = w.shape[1] // steps
            assert slab * steps == w.shape[1] and slab % BF16_SUBLANES == 0
            in_specs.append(pl.BlockSpec((None, slab, w.shape[2]), lambda i: (nxt, i, 0)))
            out_specs.append(pl.BlockSpec((slab, w.shape[2]), lambda i: (i, 0)))
            out_shape.append(jax.ShapeDtypeStruct(w.shape[1:], BF16))
            operands.append(w)
    outs = pl.pallas_call(
        functools.partial(_inproj_kernel, next_weights is not None),
        grid=(steps,),
        in_specs=in_specs,
        out_specs=out_specs,
        out_shape=out_shape,
        compiler_params=_params(1),
        name="inproj",
    )(*operands)
    return outs[0] if next_weights is None else outs


def _memkv_kernel(mem_ref, g_ref, w_ref, kg_ref, k_ref, v_ref):
    x = mem_ref[...]
    ms = jnp.mean(x * x, axis=-1, keepdims=True)
    h = (x * lax.rsqrt(ms + EPS) * g_ref[...]).astype(BF16)
    kv = jnp.dot(h, w_ref[...], preferred_element_type=F32)
    for hd in range(C_HEADS):
        sl = slice(hd * C_HEAD_DIM, (hd + 1) * C_HEAD_DIM)
        kh = kv[:, sl]
        kms = jnp.mean(kh * kh, axis=-1, keepdims=True)
        k_ref[:, sl] = (kh * lax.rsqrt(kms + EPS) * kg_ref[...]).astype(BF16)
    v_ref[...] = kv[:, D_C:].astype(BF16)


def _memkv(mem, mem_norm_g, w_mem_kv_bf16, k_norm_g):
    bsz = mem.shape[0]
    depth = w_mem_kv_bf16.shape[0]
    rows = bsz * MEM_LEN
    assert rows % MK_ROWS == 0
    out = jax.ShapeDtypeStruct((depth, rows, D_C), BF16)
    memk, memv = pl.pallas_call(
        _memkv_kernel,
        grid=(depth, rows // MK_ROWS),
        in_specs=[
            pl.BlockSpec((MK_ROWS, D_MODEL), lambda l, r: (r, 0)),
            pl.BlockSpec((None, 1, D_MODEL), lambda l, r: (l, 0, 0)),
            pl.BlockSpec((None, D_MODEL, 2 * D_C), lambda l, r: (l, 0, 0)),
            pl.BlockSpec((None, 1, C_HEAD_DIM), lambda l, r: (l, 0, 0)),
        ],
        out_specs=[
            pl.BlockSpec((None, MK_ROWS, D_C), lambda l, r: (l, r, 0)),
            pl.BlockSpec((None, MK_ROWS, D_C), lambda l, r: (l, r, 0)),
        ],
        out_shape=[out, out],
        compiler_params=_params(2),
        name="memkv",
    )(mem.reshape(rows, D_MODEL), mem_norm_g.reshape(depth, 1, D_MODEL), w_mem_kv_bf16,
      k_norm_g.reshape(depth, 1, C_HEAD_DIM))
    shape = (depth, bsz, MEM_LEN, D_C)
    return memk.reshape(shape), memv.reshape(shape)


def _mixout_kernel(u_ref, v_ref, za_ref, qc_ref, zc_ref, yb_ref, mk_ref, mv_ref,
                   w_ref, bias_ref, w_out_ref, x_ref, o_ref):
    def mix(c):
        rows = slice(c * CHUNK, (c + 1) * CHUNK)
        ya = []
        for g in range(A_GROUPS):
            sl = slice(g * A_GROUP_DIM, (g + 1) * A_GROUP_DIM)
            mixed = jnp.dot(w_ref[g], v_ref[rows, sl], preferred_element_type=F32) + bias_ref[:, sl]
            gate = u_ref[rows, sl].astype(F32) * za_ref[rows, sl].astype(F32)
            ya.append((gate * mixed).astype(BF16))
        yc = []
        for h in range(C_HEADS):
            sl = slice(h * C_HEAD_DIM, (h + 1) * C_HEAD_DIM)
            s = lax.dot_general(qc_ref[rows, sl], mk_ref[:, sl], _NT, preferred_element_type=F32)
            p = jnp.exp(s - jnp.max(s, axis=-1, keepdims=True))
            denom = jnp.sum(p, axis=-1, keepdims=True)
            o = jnp.dot(p.astype(BF16), mv_ref[:, sl], preferred_element_type=F32) / denom
            yc.append((o * zc_ref[rows, sl].astype(F32)).astype(BF16))
        return jnp.concatenate(ya + [yb_ref[rows, :]] + yc, axis=1)

    piece_rows = _piece_slices(MO_PIECES)

    def project(p, ys):
        rows = piece_rows[p]
        y = jnp.concatenate(ys, axis=0)
        o_ref[rows, :] = x_ref[rows, :] + jnp.dot(y, w_out_ref[...], preferred_element_type=F32)

    pieces = len(MO_PIECES)
    mix_piece = lambda p: [mix(c) for c in range(piece_rows[p].start // CHUNK,
                                                 piece_rows[p].stop // CHUNK)]
    ready = mix_piece(0)
    for p in range(pieces):
        nxt = mix_piece(p + 1) if p + 1 < pieces else None
        project(p, ready)
        ready = nxt


def _mixout(proj2d, yb2d, memk, memv, layer, s_len, w_tril, bias_full, w_out_bf16, x2d):
    m = x2d.shape[0]
    off_b = 3 * D_A // D_B
    blocks_per_seq = s_len // MO_ROWS

    def row_spec(width, col):
        return pl.BlockSpec((MO_ROWS, width), lambda i, c=col: (i, c))

    def const_spec(shape):
        nd = len(shape)
        return pl.BlockSpec(shape, lambda i, nd=nd: (0,) * nd)

    mem_spec = pl.BlockSpec((None, None, MEM_LEN, D_C),
                            lambda i: (layer, i // blocks_per_seq, 0, 0))
    return pl.pallas_call(
        _mixout_kernel,
        grid=(m // MO_ROWS,),
        in_specs=[
            row_spec(D_A, 0), row_spec(D_A, 1), row_spec(D_A, 2),
            row_spec(D_C, off_b + 4), row_spec(D_C, off_b + 5),
            row_spec(D_B, 0),
            mem_spec, mem_spec,
            const_spec((A_GROUPS, CHUNK, CHUNK)), const_spec((CHUNK, D_A)),
            pl.BlockSpec((D_MODEL, D_MODEL), lambda i: (0, 0)),
            row_spec(D_MODEL, 0),
        ],
        out_specs=pl.BlockSpec((MO_ROWS, D_MODEL), lambda i: (i, 0)),
        out_shape=jax.ShapeDtypeStruct((m, D_MODEL), F32),
        compiler_params=_params(1),
        name="mixout",
    )(proj2d, proj2d, proj2d, proj2d, proj2d, yb2d, memk, memv, w_tril, bias_full,
      w_out_bf16, x2d)


def _sb_kernel(q_ref, k_ref, v_ref, z_ref, tri_ref, y_ref, carry_ref, acc_ref, lb_ref, r_ref):
    blk = pl.program_id(1)
    carry_ref[...] = jnp.zeros_like(carry_ref)
    acc_ref[...] = jnp.zeros_like(acc_ref)
    row = lax.broadcasted_iota(jnp.int32, (CHUNK, CHUNK), 0)
    col = lax.broadcasted_iota(jnp.int32, (CHUNK, CHUNK), 1)
    causal = col < row
    heads = range(B_HEADS)
    sls = [slice(h * B_HEAD_DIM, (h + 1) * B_HEAD_DIM) for h in heads]

    def mask_top(x, diag):
        if not diag:
            return x
        top = jnp.where(causal, x[:CHUNK], 0.0)
        return top if x.shape[0] == CHUNK else jnp.concatenate([top, x[CHUNK:]], axis=0)

    def front(j, rows, diag):
        start = pl.multiple_of(j * CHUNK, CHUNK)
        zs = [lax.dot_general(q_ref[0, rows, sls[h]], k_ref[0, pl.ds(start, CHUNK), sls[h]], _NT,
                              preferred_element_type=F32) for h in heads]
        log_betas, hls = [], []
        for h in heads:
            z = zs[h]
            e = jnp.log(1.0 + jnp.exp2(-jnp.abs(z))) * LOG2E
            log_beta = jnp.minimum(z, 0.0) - e
            log_1mb = mask_top(log_beta - z, diag)
            log_betas.append(log_beta)
            hls.append(log_1mb.astype(BF16))
        rs = [jnp.dot(hls[h], tri_ref[...], preferred_element_type=F32) for h in heads]
        return log_betas, rs

    def back(j, rows, diag, log_betas, rs):
        start = pl.multiple_of(j * CHUNK, CHUNK)
        for h in heads:
            a = mask_top(jnp.exp2(log_betas[h] + rs[h][:, :CHUNK] + carry_ref[h, rows, :]), diag)
            acc_ref[h, rows, :] += jnp.dot(a.astype(BF16), v_ref[0, pl.ds(start, CHUNK), sls[h]],
                                           preferred_element_type=F32)
            carry_ref[h, rows, :] += rs[h][:, CHUNK:]

    first = blk * SB_Q_TILES
    diag_steps = [(first + (SB_Q_TILES - 1) - d, slice((SB_Q_TILES - 1 - d) * CHUNK, SB_Q_BLOCK))
                  for d in range(SB_Q_TILES)]
    fronts = [front(j, rows, True) for j, rows in diag_steps]
    for (j, rows), halves in zip(diag_steps, fronts):
        back(j, rows, True, *halves)

    def live(rows):
        return (jnp.max(carry_ref[:, rows, :]) > SB_DEAD_LOG2).astype(jnp.int32)

    lower, upper, whole = slice(0, SB_HALF), slice(SB_HALF, SB_Q_BLOCK), slice(0, SB_Q_BLOCK)

    def stage(j, rows):
        log_betas, rs = front(jnp.maximum(j, 0), rows, False)
        for h in heads:
            lb_ref[h, rows, :] = log_betas[h]
            r_ref[h, rows, :] = rs[h]

    def below_step(j, rows):
        rs = [r_ref[h, rows, :] for h in heads]
        after = [carry_ref[h, rows, :] + rs[h][:, CHUNK:] for h in heads]
        peak = functools.reduce(jnp.maximum, after)
        alive = [(jnp.max(peak[part, :]) > SB_DEAD_LOG2).astype(jnp.int32)
                 for part in (lower, upper) if part.stop <= peak.shape[0]]
        back(j, rows, False, [lb_ref[h, rows, :] for h in heads], rs)
        stage(j - 1, rows)
        return alive + [jnp.int32(0)] * (2 - len(alive))

    def on_rows(upper_alive, fn):
        return lax.cond(upper_alive > 0, lambda: fn(whole), lambda: fn(lower))

    def first_stage(rows):
        stage(first - 1, rows)
        return 0

    upper_alive = live(upper)
    on_rows(upper_alive, first_stage)

    def below_cond(c):
        jj, lower_alive, upper_alive = c
        return jnp.logical_and(jj < first, lower_alive + upper_alive > 0)

    def below_body(c):
        jj, _, upper_alive = c
        lower_alive, upper_alive = on_rows(upper_alive, lambda rows: below_step(first - 1 - jj, rows))
        return jj + 1, lower_alive, upper_alive

    lax.while_loop(below_cond, below_body, (jnp.int32(0), live(lower), upper_alive))

    for h in heads:
        y_ref[0, :, sls[h]] = (acc_ref[h] * z_ref[0, :, sls[h]].astype(F32)).astype(y_ref.dtype)


def _sb(proj, tri):
    bsz, s_len, _ = proj.shape
    off_b = 3 * D_A // D_B

    def row_spec(col):
        return pl.BlockSpec((1, SB_Q_BLOCK, D_B), lambda b, i, c=col: (b, i, c))

    def seq_spec(col):
        return pl.BlockSpec((1, s_len, D_B), lambda b, i, c=col: (b, 0, c))

    return pl.pallas_call(
        _sb_kernel,
        grid=(bsz, s_len // SB_Q_BLOCK),
        in_specs=[
            row_spec(off_b), seq_spec(off_b + 1), seq_spec(off_b + 2), row_spec(off_b + 3),
            pl.BlockSpec((CHUNK, 2 * CHUNK), lambda b, i: (0, 0)),
        ],
        out_specs=pl.BlockSpec((1, SB_Q_BLOCK, D_B), lambda b, i: (b, i, 0)),
        out_shape=jax.ShapeDtypeStruct((bsz, s_len, D_B), BF16),
        scratch_shapes=[pltpu.VMEM((B_HEADS, SB_Q_BLOCK, B_HEAD_DIM), F32),
                        pltpu.VMEM((B_HEADS, SB_Q_BLOCK, B_HEAD_DIM), F32),
                        pltpu.VMEM((B_HEADS, SB_Q_BLOCK, CHUNK), F32),
                        pltpu.VMEM((B_HEADS, SB_Q_BLOCK, 2 * CHUNK), F32)],
        compiler_params=_params(2),
        name="sb",
    )(proj, proj, proj, proj, tri)


def _cumsum_matrix():
    j = lax.broadcasted_iota(jnp.int32, (CHUNK, CHUNK), 0)
    s = lax.broadcasted_iota(jnp.int32, (CHUNK, CHUNK), 1)
    return jnp.concatenate([(j > s).astype(BF16), jnp.ones((CHUNK, CHUNK), BF16)], axis=1)


def kernel(x, mem, norm_g, w_in, sgu_ln_g, sgu_ln_b, sgu_w, sgu_b, mem_norm_g, w_mem_kv,
           q_norm_g, k_norm_g, w_out):
    bsz, s_len, _ = x.shape
    depth = w_in.shape[0]
    assert s_len % SB_Q_BLOCK == 0 and x.shape[2] == D_MODEL and mem.shape[1] == MEM_LEN
    m = bsz * s_len

    tri = _cumsum_matrix()
    tril = jnp.tril(jnp.ones((CHUNK, CHUNK), dtype=bool))
    memk, memv = _memkv(mem, mem_norm_g, w_mem_kv.astype(BF16), k_norm_g)
    w_in_bf16 = w_in[0].astype(BF16)
    w_out_bf16 = w_out[0].astype(BF16)

    x2d = x.reshape(m, D_MODEL)
    for l in range(depth):
        outs = _inproj(x2d, norm_g[l].reshape(1, D_MODEL), w_in_bf16,
                       sgu_ln_g[l].reshape(1, D_A), sgu_ln_b[l].reshape(1, D_A),
                       q_norm_g[l].reshape(1, C_HEAD_DIM),
                       next_weights=(w_in, w_out, l + 1) if l + 1 < depth else None)
        proj, w_in_next, w_out_next = outs if l + 1 < depth else (outs, None, None)
        w_tril = jnp.where(tril[None], sgu_w[l], 0.0).astype(BF16)
        bias_full = jnp.repeat(sgu_b[l].T, A_GROUP_DIM, axis=1)
        yb = _sb(proj.reshape(bsz, s_len, IN_WIDTH), tri)
        x2d = _mixout(proj, yb.reshape(m, D_B), memk, memv, l, s_len,
                      w_tril, bias_full, w_out_bf16, x2d)
        w_in_bf16, w_out_bf16 = w_in_next, w_out_next
    return x2d.reshape(bsz, s_len, D_MODEL)
```

```python
import functools
import itertools
import math

import jax
import jax.numpy as jnp
from jax import lax
from jax.experimental import pallas as pl
from jax.experimental.pallas import tpu as pltpu

D_MODEL = 2048
MEM_LEN = 256
CHUNK = 128
D_A = D_MODEL // 2
A_GROUPS = 8
A_GROUP_DIM = D_A // A_GROUPS
D_B = D_MODEL // 4
B_HEADS = 4
B_HEAD_DIM = D_B // B_HEADS
D_C = D_MODEL // 4
C_HEADS = 4
C_HEAD_DIM = D_C // C_HEADS
IN_WIDTH = 3 * D_A + 4 * D_B + 2 * D_C
IN_SEGMENTS = (
    (0, D_A, "gelu"), (D_A, D_A, "gelu_ln"), (2 * D_A, D_A, "silu"),
    (3 * D_A, D_B, "sb_q"), (3 * D_A + D_B, D_B, "id"), (3 * D_A + 2 * D_B, D_B, "id"),
    (3 * D_A + 3 * D_B, D_B, "silu"),
    (3 * D_A + 4 * D_B, D_C, "mem_q"), (3 * D_A + 4 * D_B + D_C, D_C, "silu"),
)
EPS = 1e-6
LOG2E = math.log2(math.e)

SB_Q_TILES = 4
SB_Q_BLOCK = SB_Q_TILES * CHUNK
SB_HALF = SB_Q_BLOCK // 2
SB_DEAD_LOG2 = -150.0
IP_PIECES = (256, 256)
IP_ROWS = sum(IP_PIECES)
IP_COLS = 2048
IP_CHUNK_ORDER = (2, 0, 1)
MO_PIECES = (256, 256)
MO_ROWS = sum(MO_PIECES)
MK_ROWS = 1024
BF16_SUBLANES = 16

F32 = jnp.float32
BF16 = jnp.bfloat16

VMEM_LIMIT_BYTES = 56 * 1024 * 1024

_NT = (((1,), (1,)), ((), ()))


def _gelu(x):
    return 0.5 * x * (1.0 + lax.erf(x * math.sqrt(0.5)))


def _silu(x):
    return x / (1.0 + jnp.exp(-x))


def _piece_slices(sizes):
    assert all(size % CHUNK == 0 for size in sizes)
    stops = list(itertools.accumulate(sizes))
    return [slice(stop - size, stop) for size, stop in zip(sizes, stops)]


def _params(n_grid_dims):
    return pltpu.CompilerParams(
        dimension_semantics=("arbitrary",) * n_grid_dims,
        vmem_limit_bytes=VMEM_LIMIT_BYTES)


def _inproj_kernel(cast_next, x_ref, g_ref, w_ref, lng_ref, lnb_ref, qg_ref, *rest):
    if cast_next:
        w_in_f32_ref, w_out_f32_ref, o_ref, w_in_bf16_ref, w_out_bf16_ref = rest
        w_in_bf16_ref[...] = w_in_f32_ref[...].astype(BF16)
        w_out_bf16_ref[...] = w_out_f32_ref[...].astype(BF16)
    else:
        (o_ref,) = rest
    piece_rows = _piece_slices(IP_PIECES)

    def norm(p):
        x = x_ref[piece_rows[p], :]
        ms = jnp.mean(x * x, axis=-1, keepdims=True)
        return (x * lax.rsqrt(ms + EPS) * g_ref[...]).astype(BF16)

    def activate(kind, t):
        if kind == "gelu":
            return _gelu(t)
        if kind == "gelu_ln":
            v = _gelu(t)
            vc = v - jnp.mean(v, axis=-1, keepdims=True)
            var = jnp.mean(vc * vc, axis=-1, keepdims=True)
            return vc * lax.rsqrt(var + EPS) * lng_ref[...] + lnb_ref[...]
        if kind == "silu":
            return _silu(t)
        if kind == "sb_q":
            return t * (LOG2E / math.sqrt(B_HEAD_DIM))
        if kind == "mem_q":
            heads = []
            for h in range(C_HEADS):
                q = t[:, h * C_HEAD_DIM:(h + 1) * C_HEAD_DIM]
                qms = jnp.mean(q * q, axis=-1, keepdims=True)
                heads.append(q * lax.rsqrt(qms + EPS) * (qg_ref[...] / math.sqrt(C_HEAD_DIM)))
            return jnp.concatenate(heads, axis=1)
        assert kind == "id"
        return t

    def project(p, h):
        rows = piece_rows[p]
        for c in IP_CHUNK_ORDER:
            acc = jnp.dot(h, w_ref[:, c * IP_COLS:(c + 1) * IP_COLS], preferred_element_type=F32)
            for start, width, kind in IN_SEGMENTS:
                if c * IP_COLS <= start < (c + 1) * IP_COLS:
                    t = acc[:, start - c * IP_COLS:start - c * IP_COLS + width]
                    o_ref[rows, start:start + width] = activate(kind, t).astype(o_ref.dtype)

    pieces = len(IP_PIECES)
    ready = norm(0)
    for p in range(pieces):
        nxt = norm(p + 1) if p + 1 < pieces else None
        project(p, ready)
        ready = nxt


def _inproj(x2d, g, w_bf16, lng, lnb, qg, next_weights=None):
    m, k = x2d.shape
    n = w_bf16.shape[1]
    steps = m // IP_ROWS
    assert n == IN_WIDTH and all(w % CHUNK == 0 and s // IP_COLS == (s + w - 1) // IP_COLS
                                 for s, w, _ in IN_SEGMENTS)
    assert sorted(IP_CHUNK_ORDER) == list(range(IN_WIDTH // IP_COLS))
    in_specs = [
        pl.BlockSpec((IP_ROWS, k), lambda i: (i, 0)),
        pl.BlockSpec((1, k), lambda i: (0, 0)),
        pl.BlockSpec((k, n), lambda i: (0, 0), pipeline_mode=pl.Buffered(1)),
        pl.BlockSpec((1, D_A), lambda i: (0, 0)),
        pl.BlockSpec((1, D_A), lambda i: (0, 0)),
        pl.BlockSpec((1, C_HEAD_DIM), lambda i: (0, 0)),
    ]
    out_specs = [pl.BlockSpec((IP_ROWS, n), lambda i: (i, 0))]
    out_shape = [jax.ShapeDtypeStruct((m, n), BF16)]
    operands = [x2d, g, w_bf16, lng, lnb, qg]
    if next_weights is not None:
        w_in_f32, w_out_f32, nxt = next_weights
        for w in (w_in_f32, w_out_f32):
            slab = w.shape[1] // steps
            assert slab * steps == w.shape[1] and slab % BF16_SUBLANES == 0
            in_specs.append(pl.BlockSpec((None, slab, w.shape[2]), lambda i: (nxt, i, 0)))
            out_specs.append(pl.BlockSpec((slab, w.shape[2]), lambda i: (i, 0)))
            out_shape.append(jax.ShapeDtypeStruct(w.shape[1:], BF16))
            operands.append(w)
    outs = pl.pallas_call(
        functools.partial(_inproj_kernel, next_weights is not None),
        grid=(steps,),
        in_specs=in_specs,
        out_specs=out_specs,
        out_shape=out_shape,
        compiler_params=_params(1),
        name="inproj",
    )(*operands)
    return outs[0] if next_weights is None else outs


def _memkv_kernel(mem_ref, g_ref, w_ref, kg_ref, k_ref, v_ref):
    x = mem_ref[...]
    ms = jnp.mean(x * x, axis=-1, keepdims=True)
    h = (x * lax.rsqrt(ms + EPS) * g_ref[...]).astype(BF16)
    kv = jnp.dot(h, w_ref[...], preferred_element_type=F32)
    for hd in range(C_HEADS):
        sl = slice(hd * C_HEAD_DIM, (hd + 1) * C_HEAD_DIM)
        kh = kv[:, sl]
        kms = jnp.mean(kh * kh, axis=-1, keepdims=True)
        k_ref[:, sl] = (kh * lax.rsqrt(kms + EPS) * kg_ref[...]).astype(BF16)
    v_ref[...] = kv[:, D_C:].astype(BF16)


def _memkv(mem, mem_norm_g, w_mem_kv_bf16, k_norm_g):
    bsz = mem.shape[0]
    depth = w_mem_kv_bf16.shape[0]
    rows = bsz * MEM_LEN
    assert rows % MK_ROWS == 0
    out = jax.ShapeDtypeStruct((depth, rows, D_C), BF16)
    memk, memv = pl.pallas_call(
        _memkv_kernel,
        grid=(depth, rows // MK_ROWS),
        in_specs=[
            pl.BlockSpec((MK_ROWS, D_MODEL), lambda l, r: (r, 0)),
            pl.BlockSpec((None, 1, D_MODEL), lambda l, r: (l, 0, 0)),
            pl.BlockSpec((None, D_MODEL, 2 * D_C), lambda l, r: (l, 0, 0)),
            pl.BlockSpec((None, 1, C_HEAD_DIM), lambda l, r: (l, 0, 0)),
        ],
        out_specs=[
            pl.BlockSpec((None, MK_ROWS, D_C), lambda l, r: (l, r, 0)),
            pl.BlockSpec((None, MK_ROWS, D_C), lambda l, r: (l, r, 0)),
        ],
        out_shape=[out, out],
        compiler_params=_params(2),
        name="memkv",
    )(mem.reshape(rows, D_MODEL), mem_norm_g.reshape(depth, 1, D_MODEL), w_mem_kv_bf16,
      k_norm_g.reshape(depth, 1, C_HEAD_DIM))
    shape = (depth, bsz, MEM_LEN, D_C)
    return memk.reshape(shape), memv.reshape(shape)


def _mixout_kernel(u_ref, v_ref, za_ref, qc_ref, zc_ref, yb_ref, mk_ref, mv_ref,
                   w_ref, bias_ref, w_out_ref, x_ref, o_ref):
    def gating(c):
        rows = slice(c * CHUNK, (c + 1) * CHUNK)
        ya = []
        for g in range(A_GROUPS):
            sl = slice(g * A_GROUP_DIM, (g + 1) * A_GROUP_DIM)
            mixed = jnp.dot(w_ref[g], v_ref[rows, sl], preferred_element_type=F32) + bias_ref[:, sl]
            gate = u_ref[rows, sl].astype(F32) * za_ref[rows, sl].astype(F32)
            ya.append((gate * mixed).astype(BF16))
        return jnp.concatenate(ya, axis=1)

    def memory_attention(rows):
        yc = []
        for h in range(C_HEADS):
            sl = slice(h * C_HEAD_DIM, (h + 1) * C_HEAD_DIM)
            s = lax.dot_general(qc_ref[rows, sl], mk_ref[:, sl], _NT, preferred_element_type=F32)
            p = jnp.exp(s - jnp.max(s, axis=-1, keepdims=True))
            denom = jnp.sum(p, axis=-1, keepdims=True)
            o = jnp.dot(p.astype(BF16), mv_ref[:, sl], preferred_element_type=F32) / denom
            yc.append((o * zc_ref[rows, sl].astype(F32)).astype(BF16))
        return jnp.concatenate(yc, axis=1)

    piece_rows = _piece_slices(MO_PIECES)

    yc_all = memory_attention(slice(0, MO_ROWS))

    def mix(p):
        rows = piece_rows[p]
        ya = jnp.concatenate([gating(c) for c in range(rows.start // CHUNK, rows.stop // CHUNK)],
                             axis=0)
        return jnp.concatenate([ya, yb_ref[rows, :], yc_all[rows, :]], axis=1)

    def project(p, y):
        rows = piece_rows[p]
        o_ref[rows, :] = x_ref[rows, :] + jnp.dot(y, w_out_ref[...], preferred_element_type=F32)

    pieces = len(MO_PIECES)
    ready = mix(0)
    for p in range(pieces):
        nxt = mix(p + 1) if p + 1 < pieces else None
        project(p, ready)
        ready = nxt


def _mixout(proj2d, yb2d, memk, memv, layer, s_len, w_tril, bias_full, w_out_bf16, x2d):
    m = x2d.shape[0]
    off_b = 3 * D_A // D_B
    blocks_per_seq = s_len // MO_ROWS

    def row_spec(width, col):
        return pl.BlockSpec((MO_ROWS, width), lambda i, c=col: (i, c))

    def const_spec(shape):
        nd = len(shape)
        return pl.BlockSpec(shape, lambda i, nd=nd: (0,) * nd)

    mem_spec = pl.BlockSpec((None, None, MEM_LEN, D_C),
                            lambda i: (layer, i // blocks_per_seq, 0, 0))
    return pl.pallas_call(
        _mixout_kernel,
        grid=(m // MO_ROWS,),
        in_specs=[
            row_spec(D_A, 0), row_spec(D_A, 1), row_spec(D_A, 2),
            row_spec(D_C, off_b + 4), row_spec(D_C, off_b + 5),
            row_spec(D_B, 0),
            mem_spec, mem_spec,
            const_spec((A_GROUPS, CHUNK, CHUNK)), const_spec((CHUNK, D_A)),
            pl.BlockSpec((D_MODEL, D_MODEL), lambda i: (0, 0)),
            row_spec(D_MODEL, 0),
        ],
        out_specs=pl.BlockSpec((MO_ROWS, D_MODEL), lambda i: (i, 0)),
        out_shape=jax.ShapeDtypeStruct((m, D_MODEL), F32),
        compiler_params=_params(1),
        name="mixout",
    )(proj2d, proj2d, proj2d, proj2d, proj2d, yb2d, memk, memv, w_tril, bias_full,
      w_out_bf16, x2d)


def _sb_kernel(q_ref, k_ref, v_ref, z_ref, tri_ref, y_ref, carry_ref, acc_ref, lb_ref, r_ref):
    blk = pl.program_id(1)
    carry_ref[...] = jnp.zeros_like(carry_ref)
    acc_ref[...] = jnp.zeros_like(acc_ref)
    row = lax.broadcasted_iota(jnp.int32, (CHUNK, CHUNK), 0)
    col = lax.broadcasted_iota(jnp.int32, (CHUNK, CHUNK), 1)
    causal = col < row
    heads = range(B_HEADS)
    sls = [slice(h * B_HEAD_DIM, (h + 1) * B_HEAD_DIM) for h in heads]

    def mask_top(x, diag):
        if not diag:
            return x
        top = jnp.where(causal, x[:CHUNK], 0.0)
        return top if x.shape[0] == CHUNK else jnp.concatenate([top, x[CHUNK:]], axis=0)

    def front(j, rows, diag):
        start = pl.multiple_of(j * CHUNK, CHUNK)
        zs = [lax.dot_general(q_ref[0, rows, sls[h]], k_ref[0, pl.ds(start, CHUNK), sls[h]], _NT,
                              preferred_element_type=F32) for h in heads]
        log_betas, hls = [], []
        for h in heads:
            z = zs[h]
            e = jnp.log(1.0 + jnp.exp2(-jnp.abs(z))) * LOG2E
            log_beta = jnp.minimum(z, 0.0) - e
            log_1mb = mask_top(log_beta - z, diag)
            log_betas.append(log_beta)
            hls.append(log_1mb.astype(BF16))
        rs = [jnp.dot(hls[h], tri_ref[...], preferred_element_type=F32) for h in heads]
        return log_betas, rs

    def back(j, rows, diag, log_betas, rs):
        start = pl.multiple_of(j * CHUNK, CHUNK)
        for h in heads:
            a = mask_top(jnp.exp2(log_betas[h] + rs[h][:, :CHUNK] + carry_ref[h, rows, :]), diag)
            acc_ref[h, rows, :] += jnp.dot(a.astype(BF16), v_ref[0, pl.ds(start, CHUNK), sls[h]],
                                           preferred_element_type=F32)
            carry_ref[h, rows, :] += rs[h][:, CHUNK:]

    first = blk * SB_Q_TILES
    diag_steps = [(first + (SB_Q_TILES - 1) - d, slice((SB_Q_TILES - 1 - d) * CHUNK, SB_Q_BLOCK))
                  for d in range(SB_Q_TILES)]
    fronts = [front(j, rows, True) for j, rows in diag_steps]
    for (j, rows), halves in zip(diag_steps, fronts):
        back(j, rows, True, *halves)

    def live(rows):
        return (jnp.max(carry_ref[:, rows, :]) > SB_DEAD_LOG2).astype(jnp.int32)

    lower, upper, whole = slice(0, SB_HALF), slice(SB_HALF, SB_Q_BLOCK), slice(0, SB_Q_BLOCK)

    def stage(j, rows):
        log_betas, rs = front(jnp.maximum(j, 0), rows, False)
        for h in heads:
            lb_ref[h, rows, :] = log_betas[h]
            r_ref[h, rows, :] = rs[h]

    def below_step(j, rows):
        rs = [r_ref[h, rows, :] for h in heads]
        after = [carry_ref[h, rows, :] + rs[h][:, CHUNK:] for h in heads]
        peak = functools.reduce(jnp.maximum, after)
        alive = [(jnp.max(peak[part, :]) > SB_DEAD_LOG2).astype(jnp.int32)
                 for part in (lower, upper) if part.stop <= peak.shape[0]]
        back(j, rows, False, [lb_ref[h, rows, :] for h in heads], rs)
        stage(j - 1, rows)
        return alive + [jnp.int32(0)] * (2 - len(alive))

    def on_rows(upper_alive, fn):
        return lax.cond(upper_alive > 0, lambda: fn(whole), lambda: fn(lower))

    def first_stage(rows):
        stage(first - 1, rows)
        return 0

    upper_alive = live(upper)
    on_rows(upper_alive, first_stage)

    def below_cond(c):
        jj, lower_alive, upper_alive = c
        return jnp.logical_and(jj < first, lower_alive + upper_alive > 0)

    def below_body(c):
        jj, _, upper_alive = c
        lower_alive, upper_alive = on_rows(upper_alive, lambda rows: below_step(first - 1 - jj, rows))
        return jj + 1, lower_alive, upper_alive

    lax.while_loop(below_cond, below_body, (jnp.int32(0), live(lower), upper_alive))

    for h in heads:
        y_ref[0, :, sls[h]] = (acc_ref[h] * z_ref[0, :, sls[h]].astype(F32)).astype(y_ref.dtype)


def _sb(proj, tri):
    bsz, s_len, _ = proj.shape
    off_b = 3 * D_A // D_B

    def row_spec(col):
        return pl.BlockSpec((1, SB_Q_BLOCK, D_B), lambda b, i, c=col: (b, i, c))

    def seq_spec(col):
        return pl.BlockSpec((1, s_len, D_B), lambda b, i, c=col: (b, 0, c))

    return pl.pallas_call(
        _sb_kernel,
        grid=(bsz, s_len // SB_Q_BLOCK),
        in_specs=[
            row_spec(off_b), seq_spec(off_b + 1), seq_spec(off_b + 2), row_spec(off_b + 3),
            pl.BlockSpec((CHUNK, 2 * CHUNK), lambda b, i: (0, 0)),
        ],
        out_specs=pl.BlockSpec((1, SB_Q_BLOCK, D_B), lambda b, i: (b, i, 0)),
        out_shape=jax.ShapeDtypeStruct((bsz, s_len, D_B), BF16),
        scratch_shapes=[pltpu.VMEM((B_HEADS, SB_Q_BLOCK, B_HEAD_DIM), F32),
                        pltpu.VMEM((B_HEADS, SB_Q_BLOCK, B_HEAD_DIM), F32),
                        pltpu.VMEM((B_HEADS, SB_Q_BLOCK, CHUNK), F32),
                        pltpu.VMEM((B_HEADS, SB_Q_BLOCK, 2 * CHUNK), F32)],
        compiler_params=_params(2),
        name="sb",
    )(proj, proj, proj, proj, tri)


def _cumsum_matrix():
    j = lax.broadcasted_iota(jnp.int32, (CHUNK, CHUNK), 0)
    s = lax.broadcasted_iota(jnp.int32, (CHUNK, CHUNK), 1)
    return jnp.concatenate([(j > s).astype(BF16), jnp.ones((CHUNK, CHUNK), BF16)], axis=1)


def kernel(x, mem, norm_g, w_in, sgu_ln_g, sgu_ln_b, sgu_w, sgu_b, mem_norm_g, w_mem_kv,
           q_norm_g, k_norm_g, w_out):
    bsz, s_len, _ = x.shape
    depth = w_in.shape[0]
    assert s_len % SB_Q_BLOCK == 0 and x.shape[2] == D_MODEL and mem.shape[1] == MEM_LEN
    m = bsz * s_len

    tri = _cumsum_matrix()
    tril = jnp.tril(jnp.ones((CHUNK, CHUNK), dtype=bool))
    memk, memv = _memkv(mem, mem_norm_g, w_mem_kv.astype(BF16), k_norm_g)
    w_in_bf16 = w_in[0].astype(BF16)
    w_out_bf16 = w_out[0].astype(BF16)

    x2d = x.reshape(m, D_MODEL)
    for l in range(depth):
        outs = _inproj(x2d, norm_g[l].reshape(1, D_MODEL), w_in_bf16,
                       sgu_ln_g[l].reshape(1, D_A), sgu_ln_b[l].reshape(1, D_A),
                       q_norm_g[l].reshape(1, C_HEAD_DIM),
                       next_weights=(w_in, w_out, l + 1) if l + 1 < depth else None)
        proj, w_in_next, w_out_next = outs if l + 1 < depth else (outs, None, None)
        w_tril = jnp.where(tril[None], sgu_w[l], 0.0).astype(BF16)
        bias_full = jnp.repeat(sgu_b[l].T, A_GROUP_DIM, axis=1)
        yb = _sb(proj.reshape(bsz, s_len, IN_WIDTH), tri)
        x2d = _mixout(proj, yb.reshape(m, D_B), memk, memv, l, s_len,
                      w_tril, bias_full, w_out_bf16, x2d)
        w_in_bf16, w_out_bf16 = w_in_next, w_out_next
    return x2d.reshape(bsz, s_len, D_MODEL)
```

```python
import functools
import itertools
import math

import jax
import jax.numpy as jnp
from jax import lax
from jax.experimental import pallas as pl
from jax.experimental.pallas import tpu as pltpu

D_MODEL = 2048
MEM_LEN = 256
CHUNK = 128
D_A = D_MODEL // 2
A_GROUPS = 8
A_GROUP_DIM = D_A // A_GROUPS
D_B = D_MODEL // 4
B_HEADS = 4
B_HEAD_DIM = D_B // B_HEADS
D_C = D_MODEL // 4
C_HEADS = 4
C_HEAD_DIM = D_C // C_HEADS
IN_WIDTH = 3 * D_A + 4 * D_B + 2 * D_C
IN_SEGMENTS = (
    (0, D_A, "gelu"), (D_A, D_A, "gelu_ln"), (2 * D_A, D_A, "silu"),
    (3 * D_A, D_B, "sb_q"), (3 * D_A + D_B, D_B, "id"), (3 * D_A + 2 * D_B, D_B, "id"),
    (3 * D_A + 3 * D_B, D_B, "silu"),
    (3 * D_A + 4 * D_B, D_C, "mem_q"), (3 * D_A + 4 * D_B + D_C, D_C, "silu"),
)
EPS = 1e-6
LOG2E = math.log2(math.e)

SB_Q_TILES = 4
SB_Q_BLOCK = SB_Q_TILES * CHUNK
SB_HALF = SB_Q_BLOCK // 2
SB_DEAD_LOG2 = -150.0
IP_PIECES = (256, 256)
IP_ROWS = sum(IP_PIECES)
IP_COLS = 2048
IP_CHUNK_ORDER = (2, 0, 1)
MO_PIECES = (256, 256)
MO_ROWS = sum(MO_PIECES)
MK_ROWS = 1024
BF16_SUBLANES = 16

F32 = jnp.float32
BF16 = jnp.bfloat16

VMEM_LIMIT_BYTES = 56 * 1024 * 1024

_NT = (((1,), (1,)), ((), ()))


def _gelu(x):
    return 0.5 * x * (1.0 + lax.erf(x * math.sqrt(0.5)))


def _silu(x):
    return x / (1.0 + jnp.exp(-x))


def _piece_slices(sizes):
    assert all(size % CHUNK == 0 for size in sizes)
    stops = list(itertools.accumulate(sizes))
    return [slice(stop - size, stop) for size, stop in zip(sizes, stops)]


def _params(n_grid_dims):
    return pltpu.CompilerParams(
        dimension_semantics=("arbitrary",) * n_grid_dims,
        vmem_limit_bytes=VMEM_LIMIT_BYTES)


def _inproj_kernel(n_casts, x_ref, g_ref, w_ref, lng_ref, lnb_ref, qg_ref, *rest):
    f32_refs, o_ref, bf16_refs = rest[:n_casts], rest[n_casts], rest[n_casts + 1:]
    for src, dst in zip(f32_refs, bf16_refs, strict=True):
        dst[...] = src[...].astype(BF16)
    piece_rows = _piece_slices(IP_PIECES)

    def norm(p):
        x = x_ref[piece_rows[p], :]
        ms = jnp.mean(x * x, axis=-1, keepdims=True)
        return (x * lax.rsqrt(ms + EPS) * g_ref[...]).astype(BF16)

    def activate(kind, t):
        if kind == "gelu":
            return _gelu(t)
        if kind == "gelu_ln":
            v = _gelu(t)
            vc = v - jnp.mean(v, axis=-1, keepdims=True)
            var = jnp.mean(vc * vc, axis=-1, keepdims=True)
            return vc * lax.rsqrt(var + EPS) * lng_ref[...] + lnb_ref[...]
        if kind == "silu":
            return _silu(t)
        if kind == "sb_q":
            return t * (LOG2E / math.sqrt(B_HEAD_DIM))
        if kind == "mem_q":
            heads = []
            for h in range(C_HEADS):
                q = t[:, h * C_HEAD_DIM:(h + 1) * C_HEAD_DIM]
                qms = jnp.mean(q * q, axis=-1, keepdims=True)
                heads.append(q * lax.rsqrt(qms + EPS) * (qg_ref[...] / math.sqrt(C_HEAD_DIM)))
            return jnp.concatenate(heads, axis=1)
        assert kind == "id"
        return t

    def project(p, h):
        rows = piece_rows[p]
        for c in IP_CHUNK_ORDER:
            acc = jnp.dot(h, w_ref[:, c * IP_COLS:(c + 1) * IP_COLS], preferred_element_type=F32)
            for start, width, kind in IN_SEGMENTS:
                if c * IP_COLS <= start < (c + 1) * IP_COLS:
                    t = acc[:, start - c * IP_COLS:start - c * IP_COLS + width]
                    o_ref[rows, start:start + width] = activate(kind, t).astype(o_ref.dtype)

    pieces = len(IP_PIECES)
    ready = norm(0)
    for p in range(pieces):
        nxt = norm(p + 1) if p + 1 < pieces else None
        project(p, ready)
        ready = nxt


def _inproj(x2d, g, w_bf16, lng, lnb, qg, casts):
    m, k = x2d.shape
    n = w_bf16.shape[1]
    steps = m // IP_ROWS
    assert n == IN_WIDTH and all(w % CHUNK == 0 and s // IP_COLS == (s + w - 1) // IP_COLS
                                 for s, w, _ in IN_SEGMENTS)
    assert sorted(IP_CHUNK_ORDER) == list(range(IN_WIDTH // IP_COLS))
    in_specs = [
        pl.BlockSpec((IP_ROWS, k), lambda i: (i, 0)),
        pl.BlockSpec((1, k), lambda i: (0, 0)),
        pl.BlockSpec((k, n), lambda i: (0, 0), pipeline_mode=pl.Buffered(1)),
        pl.BlockSpec((1, D_A), lambda i: (0, 0)),
        pl.BlockSpec((1, D_A), lambda i: (0, 0)),
        pl.BlockSpec((1, C_HEAD_DIM), lambda i: (0, 0)),
    ]
    out_specs = [pl.BlockSpec((IP_ROWS, n), lambda i: (i, 0))]
    out_shape = [jax.ShapeDtypeStruct((m, n), BF16)]
    operands = [x2d, g, w_bf16, lng, lnb, qg]
    for w, layer in casts:
        slab = w.shape[1] // steps
        assert slab * steps == w.shape[1] and slab % BF16_SUBLANES == 0
        in_specs.append(pl.BlockSpec((None, slab, w.shape[2]), lambda i, layer=layer: (layer, i, 0)))
        out_specs.append(pl.BlockSpec((slab, w.shape[2]), lambda i: (i, 0)))
        out_shape.append(jax.ShapeDtypeStruct(w.shape[1:], BF16))
        operands.append(w)
    return pl.pallas_call(
        functools.partial(_inproj_kernel, len(casts)),
        grid=(steps,),
        in_specs=in_specs,
        out_specs=out_specs,
        out_shape=out_shape,
        compiler_params=_params(1),
        name="inproj",
    )(*operands)


def _memkv_kernel(mem_ref, g_ref, w_ref, kg_ref, k_ref, v_ref):
    x = mem_ref[...]
    ms = jnp.mean(x * x, axis=-1, keepdims=True)
    h = (x * lax.rsqrt(ms + EPS) * g_ref[...]).astype(BF16)
    kv = jnp.dot(h, w_ref[...], preferred_element_type=F32)
    for hd in range(C_HEADS):
        sl = slice(hd * C_HEAD_DIM, (hd + 1) * C_HEAD_DIM)
        kh = kv[:, sl]
        kms = jnp.mean(kh * kh, axis=-1, keepdims=True)
        k_ref[:, sl] = (kh * lax.rsqrt(kms + EPS) * kg_ref[...]).astype(BF16)
    v_ref[...] = kv[:, D_C:].astype(BF16)


def _memkv(mem, mem_norm_g, w_mem_kv_bf16, k_norm_g):
    bsz = mem.shape[0]
    depth = w_mem_kv_bf16.shape[0]
    rows = bsz * MEM_LEN
    assert rows % MK_ROWS == 0
    out = jax.ShapeDtypeStruct((depth, rows, D_C), BF16)
    memk, memv = pl.pallas_call(
        _memkv_kernel,
        grid=(depth, rows // MK_ROWS),
        in_specs=[
            pl.BlockSpec((MK_ROWS, D_MODEL), lambda l, r: (r, 0)),
            pl.BlockSpec((None, 1, D_MODEL), lambda l, r: (l, 0, 0)),
            pl.BlockSpec((None, D_MODEL, 2 * D_C), lambda l, r: (l, 0, 0)),
            pl.BlockSpec((None, 1, C_HEAD_DIM), lambda l, r: (l, 0, 0)),
        ],
        out_specs=[
            pl.BlockSpec((None, MK_ROWS, D_C), lambda l, r: (l, r, 0)),
            pl.BlockSpec((None, MK_ROWS, D_C), lambda l, r: (l, r, 0)),
        ],
        out_shape=[out, out],
        compiler_params=_params(2),
        name="memkv",
    )(mem.reshape(rows, D_MODEL), mem_norm_g.reshape(depth, 1, D_MODEL), w_mem_kv_bf16,
      k_norm_g.reshape(depth, 1, C_HEAD_DIM))
    shape = (depth, bsz, MEM_LEN, D_C)
    return memk.reshape(shape), memv.reshape(shape)


def _mixout_kernel(u_ref, v_ref, za_ref, qc_ref, zc_ref, yb_ref, mk_ref, mv_ref,
                   w_ref, bias_ref, w_out_ref, x_ref, o_ref):
    def gating(c):
        rows = slice(c * CHUNK, (c + 1) * CHUNK)
        ya = []
        for g in range(A_GROUPS):
            sl = slice(g * A_GROUP_DIM, (g + 1) * A_GROUP_DIM)
            mixed = jnp.dot(w_ref[g], v_ref[rows, sl], preferred_element_type=F32) + bias_ref[:, sl]
            gate = u_ref[rows, sl].astype(F32) * za_ref[rows, sl].astype(F32)
            ya.append((gate * mixed).astype(BF16))
        return jnp.concatenate(ya, axis=1)

    def memory_attention(rows):
        yc = []
        for h in range(C_HEADS):
            sl = slice(h * C_HEAD_DIM, (h + 1) * C_HEAD_DIM)
            s = lax.dot_general(qc_ref[rows, sl], mk_ref[:, sl], _NT, preferred_element_type=F32)
            p = jnp.exp(s - jnp.max(s, axis=-1, keepdims=True))
            denom = jnp.sum(p, axis=-1, keepdims=True)
            o = jnp.dot(p.astype(BF16), mv_ref[:, sl], preferred_element_type=F32) / denom
            yc.append((o * zc_ref[rows, sl].astype(F32)).astype(BF16))
        return jnp.concatenate(yc, axis=1)

    piece_rows = _piece_slices(MO_PIECES)

    yc_all = memory_attention(slice(0, MO_ROWS))

    def mix(p):
        rows = piece_rows[p]
        ya = jnp.concatenate([gating(c) for c in range(rows.start // CHUNK, rows.stop // CHUNK)],
                             axis=0)
        return jnp.concatenate([ya, yb_ref[rows, :], yc_all[rows, :]], axis=1)

    def project(p, y):
        rows = piece_rows[p]
        o_ref[rows, :] = x_ref[rows, :] + jnp.dot(y, w_out_ref[...], preferred_element_type=F32)

    pieces = len(MO_PIECES)
    ready = mix(0)
    for p in range(pieces):
        nxt = mix(p + 1) if p + 1 < pieces else None
        project(p, ready)
        ready = nxt


def _mixout(proj2d, yb2d, memk, memv, layer, s_len, w_tril, bias_full, w_out_bf16, x2d):
    m = x2d.shape[0]
    off_b = 3 * D_A // D_B
    blocks_per_seq = s_len // MO_ROWS

    def row_spec(width, col):
        return pl.BlockSpec((MO_ROWS, width), lambda i, c=col: (i, c))

    def const_spec(shape):
        nd = len(shape)
        return pl.BlockSpec(shape, lambda i, nd=nd: (0,) * nd)

    mem_spec = pl.BlockSpec((None, None, MEM_LEN, D_C),
                            lambda i: (layer, i // blocks_per_seq, 0, 0))
    return pl.pallas_call(
        _mixout_kernel,
        grid=(m // MO_ROWS,),
        in_specs=[
            row_spec(D_A, 0), row_spec(D_A, 1), row_spec(D_A, 2),
            row_spec(D_C, off_b + 4), row_spec(D_C, off_b + 5),
            row_spec(D_B, 0),
            mem_spec, mem_spec,
            const_spec((A_GROUPS, CHUNK, CHUNK)), const_spec((CHUNK, D_A)),
            pl.BlockSpec((D_MODEL, D_MODEL), lambda i: (0, 0)),
            row_spec(D_MODEL, 0),
        ],
        out_specs=pl.BlockSpec((MO_ROWS, D_MODEL), lambda i: (i, 0)),
        out_shape=jax.ShapeDtypeStruct((m, D_MODEL), F32),
        compiler_params=_params(1),
        name="mixout",
    )(proj2d, proj2d, proj2d, proj2d, proj2d, yb2d, memk, memv, w_tril, bias_full,
      w_out_bf16, x2d)


def _sb_kernel(q_ref, k_ref, v_ref, z_ref, tri_ref, y_ref, carry_ref, acc_ref, lb_ref, r_ref):
    blk = pl.program_id(1)
    carry_ref[...] = jnp.zeros_like(carry_ref)
    acc_ref[...] = jnp.zeros_like(acc_ref)
    row = lax.broadcasted_iota(jnp.int32, (CHUNK, CHUNK), 0)
    col = lax.broadcasted_iota(jnp.int32, (CHUNK, CHUNK), 1)
    causal = col < row
    heads = range(B_HEADS)
    sls = [slice(h * B_HEAD_DIM, (h + 1) * B_HEAD_DIM) for h in heads]

    def mask_top(x, diag):
        if not diag:
            return x
        top = jnp.where(causal, x[:CHUNK], 0.0)
        return top if x.shape[0] == CHUNK else jnp.concatenate([top, x[CHUNK:]], axis=0)

    def front(j, rows, diag):
        start = pl.multiple_of(j * CHUNK, CHUNK)
        zs = [lax.dot_general(q_ref[0, rows, sls[h]], k_ref[0, pl.ds(start, CHUNK), sls[h]], _NT,
                              preferred_element_type=F32) for h in heads]
        log_betas, hls = [], []
        for h in heads:
            z = zs[h]
            e = jnp.log(1.0 + jnp.exp2(-jnp.abs(z))) * LOG2E
            log_beta = jnp.minimum(z, 0.0) - e
            log_1mb = mask_top(log_beta - z, diag)
            log_betas.append(log_beta)
            hls.append(log_1mb.astype(BF16))
        rs = [jnp.dot(hls[h], tri_ref[...], preferred_element_type=F32) for h in heads]
        return log_betas, rs

    def back(j, rows, diag, log_betas, rs):
        start = pl.multiple_of(j * CHUNK, CHUNK)
        for h in heads:
            a = mask_top(jnp.exp2(log_betas[h] + rs[h][:, :CHUNK] + carry_ref[h, rows, :]), diag)
            acc_ref[h, rows, :] += jnp.dot(a.astype(BF16), v_ref[0, pl.ds(start, CHUNK), sls[h]],
                                           preferred_element_type=F32)
            carry_ref[h, rows, :] += rs[h][:, CHUNK:]

    first = blk * SB_Q_TILES
    diag_steps = [(first + (SB_Q_TILES - 1) - d, slice((SB_Q_TILES - 1 - d) * CHUNK, SB_Q_BLOCK))
                  for d in range(SB_Q_TILES)]
    fronts = [front(j, rows, True) for j, rows in diag_steps]
    for (j, rows), halves in zip(diag_steps, fronts):
        back(j, rows, True, *halves)

    def live(rows):
        return (jnp.max(carry_ref[:, rows, :]) > SB_DEAD_LOG2).astype(jnp.int32)

    lower, upper, whole = slice(0, SB_HALF), slice(SB_HALF, SB_Q_BLOCK), slice(0, SB_Q_BLOCK)

    def stage(j, rows):
        log_betas, rs = front(jnp.maximum(j, 0), rows, False)
        for h in heads:
            lb_ref[h, rows, :] = log_betas[h]
            r_ref[h, rows, :] = rs[h]

    def below_step(j, rows):
        rs = [r_ref[h, rows, :] for h in heads]
        after = [carry_ref[h, rows, :] + rs[h][:, CHUNK:] for h in heads]
        peak = functools.reduce(jnp.maximum, after)
        alive = [(jnp.max(peak[part, :]) > SB_DEAD_LOG2).astype(jnp.int32)
                 for part in (lower, upper) if part.stop <= peak.shape[0]]
        back(j, rows, False, [lb_ref[h, rows, :] for h in heads], rs)
        stage(j - 1, rows)
        return alive + [jnp.int32(0)] * (2 - len(alive))

    def on_rows(upper_alive, fn):
        return lax.cond(upper_alive > 0, lambda: fn(whole), lambda: fn(lower))

    def first_stage(rows):
        stage(first - 1, rows)
        return 0

    upper_alive = live(upper)
    on_rows(upper_alive, first_stage)

    def below_cond(c):
        jj, lower_alive, upper_alive = c
        return jnp.logical_and(jj < first, lower_alive + upper_alive > 0)

    def below_body(c):
        jj, _, upper_alive = c
        lower_alive, upper_alive = on_rows(upper_alive, lambda rows: below_step(first - 1 - jj, rows))
        return jj + 1, lower_alive, upper_alive

    lax.while_loop(below_cond, below_body, (jnp.int32(0), live(lower), upper_alive))

    for h in heads:
        y_ref[0, :, sls[h]] = (acc_ref[h] * z_ref[0, :, sls[h]].astype(F32)).astype(y_ref.dtype)


def _sb(proj, tri):
    bsz, s_len, _ = proj.shape
    off_b = 3 * D_A // D_B

    def row_spec(col):
        return pl.BlockSpec((1, SB_Q_BLOCK, D_B), lambda b, i, c=col: (b, i, c))

    def seq_spec(col):
        return pl.BlockSpec((1, s_len, D_B), lambda b, i, c=col: (b, 0, c))

    return pl.pallas_call(
        _sb_kernel,
        grid=(bsz, s_len // SB_Q_BLOCK),
        in_specs=[
            row_spec(off_b), seq_spec(off_b + 1), seq_spec(off_b + 2), row_spec(off_b + 3),
            pl.BlockSpec((CHUNK, 2 * CHUNK), lambda b, i: (0, 0)),
        ],
        out_specs=pl.BlockSpec((1, SB_Q_BLOCK, D_B), lambda b, i: (b, i, 0)),
        out_shape=jax.ShapeDtypeStruct((bsz, s_len, D_B), BF16),
        scratch_shapes=[pltpu.VMEM((B_HEADS, SB_Q_BLOCK, B_HEAD_DIM), F32),
                        pltpu.VMEM((B_HEADS, SB_Q_BLOCK, B_HEAD_DIM), F32),
                        pltpu.VMEM((B_HEADS, SB_Q_BLOCK, CHUNK), F32),
                        pltpu.VMEM((B_HEADS, SB_Q_BLOCK, 2 * CHUNK), F32)],
        compiler_params=_params(2),
        name="sb",
    )(proj, proj, proj, proj, tri)


def _cumsum_matrix():
    j = lax.broadcasted_iota(jnp.int32, (CHUNK, CHUNK), 0)
    s = lax.broadcasted_iota(jnp.int32, (CHUNK, CHUNK), 1)
    return jnp.concatenate([(j > s).astype(BF16), jnp.ones((CHUNK, CHUNK), BF16)], axis=1)


def kernel(x, mem, norm_g, w_in, sgu_ln_g, sgu_ln_b, sgu_w, sgu_b, mem_norm_g, w_mem_kv,
           q_norm_g, k_norm_g, w_out):
    bsz, s_len, _ = x.shape
    depth = w_in.shape[0]
    assert s_len % SB_Q_BLOCK == 0 and x.shape[2] == D_MODEL and mem.shape[1] == MEM_LEN
    m = bsz * s_len

    tri = _cumsum_matrix()
    tril = jnp.tril(jnp.ones((CHUNK, CHUNK), dtype=bool))
    memk, memv = _memkv(mem, mem_norm_g, w_mem_kv.astype(BF16), k_norm_g)
    w_in_bf16 = w_in[0].astype(BF16)

    x2d = x.reshape(m, D_MODEL)
    for l in range(depth):
        casts = [(w_out, l)] + ([(w_in, l + 1)] if l + 1 < depth else [])
        proj, w_out_bf16, *w_in_next = _inproj(
            x2d, norm_g[l].reshape(1, D_MODEL), w_in_bf16,
            sgu_ln_g[l].reshape(1, D_A), sgu_ln_b[l].reshape(1, D_A),
            q_norm_g[l].reshape(1, C_HEAD_DIM), casts)
        w_tril = jnp.where(tril[None], sgu_w[l], 0.0).astype(BF16)
        bias_full = jnp.repeat(sgu_b[l].T, A_GROUP_DIM, axis=1)
        yb = _sb(proj.reshape(bsz, s_len, IN_WIDTH), tri)
        x2d = _mixout(proj, yb.reshape(m, D_B), memk, memv, l, s_len,
                      w_tril, bias_full, w_out_bf16, x2d)
        w_in_bf16 = w_in_next[0] if w_in_next else None
    return x2d.reshape(bsz, s_len, D_MODEL)
```

```python
import functools
import itertools
import math

import jax
import jax.numpy as jnp
from jax import lax
from jax.experimental import pallas as pl
from jax.experimental.pallas import tpu as pltpu

D_MODEL = 2048
MEM_LEN = 256
CHUNK = 128
D_A = D_MODEL // 2
A_GROUPS = 8
A_GROUP_DIM = D_A // A_GROUPS
D_B = D_MODEL // 4
B_HEADS = 4
B_HEAD_DIM = D_B // B_HEADS
D_C = D_MODEL // 4
C_HEADS = 4
C_HEAD_DIM = D_C // C_HEADS
IN_WIDTH = 3 * D_A + 4 * D_B + 2 * D_C
IN_SEGMENTS = (
    (0, D_A, "gelu"), (D_A, D_A, "gelu_ln"), (2 * D_A, D_A, "silu"),
    (3 * D_A, D_B, "sb_q"), (3 * D_A + D_B, D_B, "id"), (3 * D_A + 2 * D_B, D_B, "id"),
    (3 * D_A + 3 * D_B, D_B, "silu"),
    (3 * D_A + 4 * D_B, D_C, "mem_q"), (3 * D_A + 4 * D_B + D_C, D_C, "silu"),
)
EPS = 1e-6
LOG2E = math.log2(math.e)

SB_Q_TILES = 4
SB_Q_BLOCK = SB_Q_TILES * CHUNK
SB_HALF = SB_Q_BLOCK // 2
SB_DEAD_LOG2 = -150.0
IP_PIECES = (256, 256)
IP_ROWS = sum(IP_PIECES)
IP_COLS = 2048
IP_CHUNK_ORDER = (2, 0, 1)
MO_PIECES = (256, 256)
MO_ROWS = sum(MO_PIECES)
MK_ROWS = 1024
BF16_SUBLANES = 16

F32 = jnp.float32
BF16 = jnp.bfloat16

VMEM_LIMIT_BYTES = 56 * 1024 * 1024

_NT = (((1,), (1,)), ((), ()))


def _gelu(x):
    return 0.5 * x * (1.0 + lax.erf(x * math.sqrt(0.5)))


def _silu(x):
    return x / (1.0 + jnp.exp(-x))


def _piece_slices(sizes):
    assert all(size % CHUNK == 0 for size in sizes)
    stops = list(itertools.accumulate(sizes))
    return [slice(stop - size, stop) for size, stop in zip(sizes, stops)]


def _params(n_grid_dims):
    return pltpu.CompilerParams(
        dimension_semantics=("arbitrary",) * n_grid_dims,
        vmem_limit_bytes=VMEM_LIMIT_BYTES)


def _inproj_kernel(n_casts, x_ref, g_ref, w_ref, lng_ref, lnb_ref, qg_ref, *rest):
    f32_refs, o_ref, bf16_refs = rest[:n_casts], rest[n_casts], rest[n_casts + 1:]
    for src, dst in zip(f32_refs, bf16_refs, strict=True):
        dst[...] = src[...].astype(BF16)
    piece_rows = _piece_slices(IP_PIECES)

    def norm(p):
        x = x_ref[piece_rows[p], :]
        ms = jnp.mean(x * x, axis=-1, keepdims=True)
        return (x * lax.rsqrt(ms + EPS) * g_ref[...]).astype(BF16)

    def activate(kind, t):
        if kind == "gelu":
            return _gelu(t)
        if kind == "gelu_ln":
            v = _gelu(t)
            vc = v - jnp.mean(v, axis=-1, keepdims=True)
            var = jnp.mean(vc * vc, axis=-1, keepdims=True)
            return vc * lax.rsqrt(var + EPS) * lng_ref[...] + lnb_ref[...]
        if kind == "silu":
            return _silu(t)
        if kind == "sb_q":
            return t * (LOG2E / math.sqrt(B_HEAD_DIM))
        if kind == "mem_q":
            heads = []
            for h in range(C_HEADS):
                q = t[:, h * C_HEAD_DIM:(h + 1) * C_HEAD_DIM]
                qms = jnp.mean(q * q, axis=-1, keepdims=True)
                heads.append(q * lax.rsqrt(qms + EPS) * (qg_ref[...] / math.sqrt(C_HEAD_DIM)))
            return jnp.concatenate(heads, axis=1)
        assert kind == "id"
        return t

    def project(p, h):
        rows = piece_rows[p]
        for c in IP_CHUNK_ORDER:
            acc = jnp.dot(h, w_ref[:, c * IP_COLS:(c + 1) * IP_COLS], preferred_element_type=F32)
            for start, width, kind in IN_SEGMENTS:
                if c * IP_COLS <= start < (c + 1) * IP_COLS:
                    t = acc[:, start - c * IP_COLS:start - c * IP_COLS + width]
                    o_ref[rows, start:start + width] = activate(kind, t).astype(o_ref.dtype)

    pieces = len(IP_PIECES)
    ready = norm(0)
    for p in range(pieces):
        nxt = norm(p + 1) if p + 1 < pieces else None
        project(p, ready)
        ready = nxt


def _inproj(x2d, g, w_bf16, lng, lnb, qg, casts):
    m, k = x2d.shape
    n = w_bf16.shape[1]
    steps = m // IP_ROWS
    assert n == IN_WIDTH and all(w % CHUNK == 0 and s // IP_COLS == (s + w - 1) // IP_COLS
                                 for s, w, _ in IN_SEGMENTS)
    assert sorted(IP_CHUNK_ORDER) == list(range(IN_WIDTH // IP_COLS))
    in_specs = [
        pl.BlockSpec((IP_ROWS, k), lambda i: (i, 0)),
        pl.BlockSpec((1, k), lambda i: (0, 0)),
        pl.BlockSpec((k, n), lambda i: (0, 0), pipeline_mode=pl.Buffered(1)),
        pl.BlockSpec((1, D_A), lambda i: (0, 0)),
        pl.BlockSpec((1, D_A), lambda i: (0, 0)),
        pl.BlockSpec((1, C_HEAD_DIM), lambda i: (0, 0)),
    ]
    out_specs = [pl.BlockSpec((IP_ROWS, n), lambda i: (i, 0))]
    out_shape = [jax.ShapeDtypeStruct((m, n), BF16)]
    operands = [x2d, g, w_bf16, lng, lnb, qg]
    for w, layer in casts:
        slab = w.shape[1] // steps
        assert slab * steps == w.shape[1] and slab % BF16_SUBLANES == 0
        in_specs.append(pl.BlockSpec((None, slab, w.shape[2]), lambda i, layer=layer: (layer, i, 0)))
        out_specs.append(pl.BlockSpec((slab, w.shape[2]), lambda i: (i, 0)))
        out_shape.append(jax.ShapeDtypeStruct(w.shape[1:], BF16))
        operands.append(w)
    return pl.pallas_call(
        functools.partial(_inproj_kernel, len(casts)),
        grid=(steps,),
        in_specs=in_specs,
        out_specs=out_specs,
        out_shape=out_shape,
        compiler_params=_params(1),
        name="inproj",
    )(*operands)


def _memkv_kernel(mem_ref, g_ref, w_ref, kg_ref, k_ref, v_ref):
    x = mem_ref[...]
    ms = jnp.mean(x * x, axis=-1, keepdims=True)
    h = (x * lax.rsqrt(ms + EPS) * g_ref[...]).astype(BF16)
    kv = jnp.dot(h, w_ref[...].astype(BF16), preferred_element_type=F32)
    for hd in range(C_HEADS):
        sl = slice(hd * C_HEAD_DIM, (hd + 1) * C_HEAD_DIM)
        kh = kv[:, sl]
        kms = jnp.mean(kh * kh, axis=-1, keepdims=True)
        k_ref[:, sl] = (kh * lax.rsqrt(kms + EPS) * kg_ref[...]).astype(BF16)
    v_ref[...] = kv[:, D_C:].astype(BF16)


def _memkv(mem, mem_norm_g, w_mem_kv, k_norm_g):
    bsz = mem.shape[0]
    depth = w_mem_kv.shape[0]
    rows = bsz * MEM_LEN
    assert rows % MK_ROWS == 0
    out = jax.ShapeDtypeStruct((depth, rows, D_C), BF16)
    memk, memv = pl.pallas_call(
        _memkv_kernel,
        grid=(depth, rows // MK_ROWS),
        in_specs=[
            pl.BlockSpec((MK_ROWS, D_MODEL), lambda l, r: (r, 0)),
            pl.BlockSpec((None, 1, D_MODEL), lambda l, r: (l, 0, 0)),
            pl.BlockSpec((None, D_MODEL, 2 * D_C), lambda l, r: (l, 0, 0)),
            pl.BlockSpec((None, 1, C_HEAD_DIM), lambda l, r: (l, 0, 0)),
        ],
        out_specs=[
            pl.BlockSpec((None, MK_ROWS, D_C), lambda l, r: (l, r, 0)),
            pl.BlockSpec((None, MK_ROWS, D_C), lambda l, r: (l, r, 0)),
        ],
        out_shape=[out, out],
        compiler_params=_params(2),
        name="memkv",
    )(mem.reshape(rows, D_MODEL), mem_norm_g.reshape(depth, 1, D_MODEL), w_mem_kv,
      k_norm_g.reshape(depth, 1, C_HEAD_DIM))
    shape = (depth, bsz, MEM_LEN, D_C)
    return memk.reshape(shape), memv.reshape(shape)


def _mixout_kernel(u_ref, v_ref, za_ref, qc_ref, zc_ref, yb_ref, mk_ref, mv_ref,
                   w_ref, bias_ref, w_out_ref, x_ref, o_ref):
    def gating(c):
        rows = slice(c * CHUNK, (c + 1) * CHUNK)
        ya = []
        for g in range(A_GROUPS):
            sl = slice(g * A_GROUP_DIM, (g + 1) * A_GROUP_DIM)
            mixed = jnp.dot(w_ref[g], v_ref[rows, sl], preferred_element_type=F32) + bias_ref[:, sl]
            gate = u_ref[rows, sl].astype(F32) * za_ref[rows, sl].astype(F32)
            ya.append((gate * mixed).astype(BF16))
        return jnp.concatenate(ya, axis=1)

    def memory_attention(rows):
        yc = []
        for h in range(C_HEADS):
            sl = slice(h * C_HEAD_DIM, (h + 1) * C_HEAD_DIM)
            s = lax.dot_general(qc_ref[rows, sl], mk_ref[:, sl], _NT, preferred_element_type=F32)
            p = jnp.exp(s - jnp.max(s, axis=-1, keepdims=True))
            denom = jnp.sum(p, axis=-1, keepdims=True)
            o = jnp.dot(p.astype(BF16), mv_ref[:, sl], preferred_element_type=F32) / denom
            yc.append((o * zc_ref[rows, sl].astype(F32)).astype(BF16))
        return jnp.concatenate(yc, axis=1)

    piece_rows = _piece_slices(MO_PIECES)

    yc_all = memory_attention(slice(0, MO_ROWS))

    def mix(p):
        rows = piece_rows[p]
        ya = jnp.concatenate([gating(c) for c in range(rows.start // CHUNK, rows.stop // CHUNK)],
                             axis=0)
        return jnp.concatenate([ya, yb_ref[rows, :], yc_all[rows, :]], axis=1)

    def project(p, y):
        rows = piece_rows[p]
        o_ref[rows, :] = x_ref[rows, :] + jnp.dot(y, w_out_ref[...], preferred_element_type=F32)

    pieces = len(MO_PIECES)
    ready = mix(0)
    for p in range(pieces):
        nxt = mix(p + 1) if p + 1 < pieces else None
        project(p, ready)
        ready = nxt


def _mixout(proj2d, yb2d, memk, memv, layer, s_len, w_tril, bias_full, w_out_bf16, x2d):
    m = x2d.shape[0]
    off_b = 3 * D_A // D_B
    blocks_per_seq = s_len // MO_ROWS

    def row_spec(width, col):
        return pl.BlockSpec((MO_ROWS, width), lambda i, c=col: (i, c))

    def const_spec(shape):
        nd = len(shape)
        return pl.BlockSpec(shape, lambda i, nd=nd: (0,) * nd)

    mem_spec = pl.BlockSpec((None, None, MEM_LEN, D_C),
                            lambda i: (layer, i // blocks_per_seq, 0, 0))
    return pl.pallas_call(
        _mixout_kernel,
        grid=(m // MO_ROWS,),
        in_specs=[
            row_spec(D_A, 0), row_spec(D_A, 1), row_spec(D_A, 2),
            row_spec(D_C, off_b + 4), row_spec(D_C, off_b + 5),
            row_spec(D_B, 0),
            mem_spec, mem_spec,
            const_spec((A_GROUPS, CHUNK, CHUNK)), const_spec((CHUNK, D_A)),
            pl.BlockSpec((D_MODEL, D_MODEL), lambda i: (0, 0)),
            row_spec(D_MODEL, 0),
        ],
        out_specs=pl.BlockSpec((MO_ROWS, D_MODEL), lambda i: (i, 0)),
        out_shape=jax.ShapeDtypeStruct((m, D_MODEL), F32),
        compiler_params=_params(1),
        name="mixout",
    )(proj2d, proj2d, proj2d, proj2d, proj2d, yb2d, memk, memv, w_tril, bias_full,
      w_out_bf16, x2d)


def _sb_kernel(q_ref, k_ref, v_ref, z_ref, tri_ref, y_ref, carry_ref, acc_ref, lb_ref, r_ref):
    blk = pl.program_id(1)
    carry_ref[...] = jnp.zeros_like(carry_ref)
    acc_ref[...] = jnp.zeros_like(acc_ref)
    row = lax.broadcasted_iota(jnp.int32, (CHUNK, CHUNK), 0)
    col = lax.broadcasted_iota(jnp.int32, (CHUNK, CHUNK), 1)
    causal = col < row
    heads = range(B_HEADS)
    sls = [slice(h * B_HEAD_DIM, (h + 1) * B_HEAD_DIM) for h in heads]

    def mask_top(x, diag):
        if not diag:
            return x
        top = jnp.where(causal, x[:CHUNK], 0.0)
        return top if x.shape[0] == CHUNK else jnp.concatenate([top, x[CHUNK:]], axis=0)

    def front(j, rows, diag):
        start = pl.multiple_of(j * CHUNK, CHUNK)
        zs = [lax.dot_general(q_ref[0, rows, sls[h]], k_ref[0, pl.ds(start, CHUNK), sls[h]], _NT,
                              preferred_element_type=F32) for h in heads]
        log_betas, hls = [], []
        for h in heads:
            z = zs[h]
            e = jnp.log(1.0 + jnp.exp2(-jnp.abs(z))) * LOG2E
            log_beta = jnp.minimum(z, 0.0) - e
            log_1mb = mask_top(log_beta - z, diag)
            log_betas.append(log_beta)
            hls.append(log_1mb.astype(BF16))
        rs = [jnp.dot(hls[h], tri_ref[...], preferred_element_type=F32) for h in heads]
        return log_betas, rs

    def back(j, rows, diag, log_betas, rs):
        start = pl.multiple_of(j * CHUNK, CHUNK)
        for h in heads:
            a = mask_top(jnp.exp2(log_betas[h] + rs[h][:, :CHUNK] + carry_ref[h, rows, :]), diag)
            acc_ref[h, rows, :] += jnp.dot(a.astype(BF16), v_ref[0, pl.ds(start, CHUNK), sls[h]],
                                           preferred_element_type=F32)
            carry_ref[h, rows, :] += rs[h][:, CHUNK:]

    first = blk * SB_Q_TILES
    diag_steps = [(first + (SB_Q_TILES - 1) - d, slice((SB_Q_TILES - 1 - d) * CHUNK, SB_Q_BLOCK))
                  for d in range(SB_Q_TILES)]
    fronts = [front(j, rows, True) for j, rows in diag_steps]
    for (j, rows), halves in zip(diag_steps, fronts):
        back(j, rows, True, *halves)

    def live(rows):
        return (jnp.max(carry_ref[:, rows, :]) > SB_DEAD_LOG2).astype(jnp.int32)

    lower, upper, whole = slice(0, SB_HALF), slice(SB_HALF, SB_Q_BLOCK), slice(0, SB_Q_BLOCK)

    dead = [jnp.int32(0), jnp.int32(0)]

    def stage(j, rows, carry_after):
        log_betas, rs = front(j, rows, False)
        for h in heads:
            lb_ref[h, rows, :] = log_betas[h]
            r_ref[h, rows, :] = rs[h]
        peak = functools.reduce(jnp.maximum, [carry_after[h] + rs[h][:, CHUNK:] for h in heads])
        alive = [(jnp.max(peak[part, :]) > SB_DEAD_LOG2).astype(jnp.int32)
                 for part in (lower, upper) if part.stop <= peak.shape[0]]
        return alive + dead[len(alive):]

    def below_step(j, rows, another):
        def consume(stage_next):
            rs = [r_ref[h, rows, :] for h in heads]
            log_betas = [lb_ref[h, rows, :] for h in heads]
            carry_after = [carry_ref[h, rows, :] + rs[h][:, CHUNK:] for h in heads]
            back(j, rows, False, log_betas, rs)
            return stage(j - 1, rows, carry_after) if stage_next else dead

        return lax.cond(another, lambda: consume(True), lambda: consume(False))

    def on_rows(upper_alive, fn):
        return lax.cond(upper_alive > 0, lambda: fn(whole), lambda: fn(lower))

    def below_cond(c):
        jj, now, _ = c
        return jnp.logical_and(jj < first, now[0] + now[1] > 0)

    def below_body(c):
        jj, now, after = c
        j = first - 1 - jj
        another = jnp.logical_and(j > 0, after[0] + after[1] > 0)
        nxt = on_rows(now[1], lambda rows: below_step(j, rows, another))
        return jj + 1, after, nxt

    now = [live(lower), live(upper)]
    after = lax.cond(
        below_cond((jnp.int32(0), now, dead)),
        lambda: on_rows(now[1], lambda rows: stage(first - 1, rows,
                                                   [carry_ref[h, rows, :] for h in heads])),
        lambda: dead)
    lax.while_loop(below_cond, below_body, (jnp.int32(0), now, after))

    for h in heads:
        y_ref[0, :, sls[h]] = (acc_ref[h] * z_ref[0, :, sls[h]].astype(F32)).astype(y_ref.dtype)


def _sb(proj, tri):
    bsz, s_len, _ = proj.shape
    off_b = 3 * D_A // D_B

    def row_spec(col):
        return pl.BlockSpec((1, SB_Q_BLOCK, D_B), lambda b, i, c=col: (b, i, c))

    def seq_spec(col):
        return pl.BlockSpec((1, s_len, D_B), lambda b, i, c=col: (b, 0, c))

    return pl.pallas_call(
        _sb_kernel,
        grid=(bsz, s_len // SB_Q_BLOCK),
        in_specs=[
            row_spec(off_b), seq_spec(off_b + 1), seq_spec(off_b + 2), row_spec(off_b + 3),
            pl.BlockSpec((CHUNK, 2 * CHUNK), lambda b, i: (0, 0)),
        ],
        out_specs=pl.BlockSpec((1, SB_Q_BLOCK, D_B), lambda b, i: (b, i, 0)),
        out_shape=jax.ShapeDtypeStruct((bsz, s_len, D_B), BF16),
        scratch_shapes=[pltpu.VMEM((B_HEADS, SB_Q_BLOCK, B_HEAD_DIM), F32),
                        pltpu.VMEM((B_HEADS, SB_Q_BLOCK, B_HEAD_DIM), F32),
                        pltpu.VMEM((B_HEADS, SB_Q_BLOCK, CHUNK), F32),
                        pltpu.VMEM((B_HEADS, SB_Q_BLOCK, 2 * CHUNK), F32)],
        compiler_params=_params(2),
        name="sb",
    )(proj, proj, proj, proj, tri)


def _cumsum_matrix():
    j = lax.broadcasted_iota(jnp.int32, (CHUNK, CHUNK), 0)
    s = lax.broadcasted_iota(jnp.int32, (CHUNK, CHUNK), 1)
    return jnp.concatenate([(j > s).astype(BF16), jnp.ones((CHUNK, CHUNK), BF16)], axis=1)


def kernel(x, mem, norm_g, w_in, sgu_ln_g, sgu_ln_b, sgu_w, sgu_b, mem_norm_g, w_mem_kv,
           q_norm_g, k_norm_g, w_out):
    bsz, s_len, _ = x.shape
    depth = w_in.shape[0]
    assert s_len % SB_Q_BLOCK == 0 and x.shape[2] == D_MODEL and mem.shape[1] == MEM_LEN
    m = bsz * s_len

    tri = _cumsum_matrix()
    tril = jnp.tril(jnp.ones((CHUNK, CHUNK), dtype=bool))
    memk, memv = _memkv(mem, mem_norm_g, w_mem_kv, k_norm_g)
    w_in_bf16 = w_in[0].astype(BF16)

    x2d = x.reshape(m, D_MODEL)
    for l in range(depth):
        casts = [(w_out, l)] + ([(w_in, l + 1)] if l + 1 < depth else [])
        proj, w_out_bf16, *w_in_next = _inproj(
            x2d, norm_g[l].reshape(1, D_MODEL), w_in_bf16,
            sgu_ln_g[l].reshape(1, D_A), sgu_ln_b[l].reshape(1, D_A),
            q_norm_g[l].reshape(1, C_HEAD_DIM), casts)
        w_tril = jnp.where(tril[None], sgu_w[l], 0.0).astype(BF16)
        bias_full = jnp.repeat(sgu_b[l].T, A_GROUP_DIM, axis=1)
        yb = _sb(proj.reshape(bsz, s_len, IN_WIDTH), tri)
        x2d = _mixout(proj, yb.reshape(m, D_B), memk, memv, l, s_len,
                      w_tril, bias_full, w_out_bf16, x2d)
        w_in_bf16 = w_in_next[0] if w_in_next else None
    return x2d.reshape(bsz, s_len, D_MODEL)
```
